```python
import math
import jax, jax.numpy as jnp
from jax import lax
import numpy as np

D_MODEL = 1024
BATCH = 8
SEQ = 2048
DEPTH = 1

N_MEM = 256
DA_HEADS = 8
DA_QK_DIM = 64
DA_V_DIM = 2 * DA_QK_DIM
DA_QK_WIDTH = DA_HEADS * 2 * DA_QK_DIM
DA_V_WIDTH = DA_HEADS * DA_V_DIM
LRU_WIDTH = D_MODEL
LRU_BLOCKS = 8
LRU_BLOCK = LRU_WIDTH // LRU_BLOCKS
CONV_WIDTH = 4
LRU_C = 8.0
CA_HEADS = 4
CA_HEAD_DIM = D_MODEL // CA_HEADS
CA_WIDTH = CA_HEADS * CA_HEAD_DIM
D_FF = 2816
MACARON_WEIGHT = 0.5
N_BRANCH = 3
Q_BLOCK = 128
EPS = 1e-6
IN_WIDTHS = (DA_QK_WIDTH, DA_QK_WIDTH, DA_V_WIDTH, LRU_WIDTH, LRU_WIDTH, CA_WIDTH)
D_IN = sum(IN_WIDTHS)
IN_SPLITS = tuple(int(v) for v in np.cumsum(IN_WIDTHS)[:-1])

kernel_name = 'hybrid_diffattn_rglru_memxattn_macaron'


def rmsnorm(x, g):
    xf = x.astype(jnp.float32)
    y = xf * lax.rsqrt(jnp.mean(xf * xf, axis=-1, keepdims=True) + EPS)
    return (y * g.astype(jnp.float32)).astype(x.dtype)


def swiglu_half_step(x, pre_g, w_gate, w_up, w_down, post_g):
    h = rmsnorm(x, pre_g)
    f = (jax.nn.silu(h @ w_gate) * (h @ w_up)) @ w_down
    return x + MACARON_WEIGHT * rmsnorm(f, post_g)


def alibi_slopes(n_heads):
    return jnp.exp2(-8.0 * jnp.arange(1, n_heads + 1, dtype=jnp.float32) / n_heads)


def diff_attention(q, k, v, lam, lam_init, head_g):
    B, S = q.shape[0], q.shape[1]
    q = q.transpose(0, 2, 3, 1, 4)
    k = k.transpose(0, 2, 3, 1, 4)
    v = v.transpose(0, 2, 1, 3)
    scale = DA_QK_DIM ** -0.5
    slopes = alibi_slopes(DA_HEADS)[:, None, None, None]
    outs = []
    for start in range(0, S, Q_BLOCK):
        end = start + Q_BLOCK
        qb = q[:, :, :, start:end]
        kb = k[:, :, :, :end]
        vb = v[:, :, :end]
        s = jnp.einsum('bhmqd,bhmkd->bhmqk', qb, kb).astype(jnp.float32) * scale
        dist = (jnp.arange(start, end)[:, None] - jnp.arange(end)[None, :]).astype(jnp.float32)
        s = jnp.where(dist >= 0.0, s - slopes * dist, -jnp.inf)
        p = jax.nn.softmax(s, axis=-1)
        a = p[:, :, 0] - lam * p[:, :, 1]
        outs.append(jnp.einsum('bhqk,bhkd->bhqd', a.astype(vb.dtype), vb))
    o = jnp.concatenate(outs, axis=2)
    o = rmsnorm(o, head_g) * (1.0 - lam_init)
    return o.transpose(0, 2, 1, 3).reshape(B, S, DA_V_WIDTH)


def rg_lru_branch(xr, yr, conv_w, conv_b, w_a, b_a, w_x, b_x, lam):
    B, S, W = xr.shape
    xp = jnp.pad(xr, ((0, 0), (CONV_WIDTH - 1, 0), (0, 0)))
    xc = conv_b + sum(xp[:, t:t + S] * conv_w[t] for t in range(CONV_WIDTH))
    xb = xc.reshape(B, S, LRU_BLOCKS, LRU_BLOCK)
    r = jax.nn.sigmoid(jnp.einsum('bsnc,ncd->bsnd', xb, w_a).reshape(B, S, W) + b_a)
    i = jax.nn.sigmoid(jnp.einsum('bsnc,ncd->bsnd', xb, w_x).reshape(B, S, W) + b_x)
    log_a = -LRU_C * r.astype(jnp.float32) * jax.nn.softplus(-lam.astype(jnp.float32))
    a = jnp.exp(log_a)
    u = jnp.sqrt(-jnp.expm1(2.0 * log_a)) * (i * xc).astype(jnp.float32)

    def combine(left, right):
        a1, b1 = left
        a2, b2 = right
        return a1 * a2, a2 * b1 + b2

    _, h = lax.associative_scan(combine, (a, u), axis=1)
    return h.astype(xr.dtype) * jax.nn.gelu(yr)


def memory_cross_attention(qc, mem_n, w_mem_kv):
    B, S = qc.shape[0], qc.shape[1]
    M = mem_n.shape[1]
    kv = mem_n @ w_mem_kv
    km, vm = jnp.split(kv, 2, axis=-1)
    q = qc.reshape(B, S, CA_HEADS, CA_HEAD_DIM)
    km = km.reshape(B, M, CA_HEADS, CA_HEAD_DIM)
    vm = vm.reshape(B, M, CA_HEADS, CA_HEAD_DIM)
    s = jnp.einsum('bshd,bmhd->bhsm', q, km).astype(jnp.float32) * (CA_HEAD_DIM ** -0.5)
    p = jax.nn.softmax(s, axis=-1)
    o = jnp.einsum('bhsm,bmhd->bshd', p.astype(vm.dtype), vm)
    return o.reshape(B, S, CA_WIDTH)


def setup_inputs(seed: int = 0) -> dict:
    key = jax.random.key(seed)
    ks = iter(jax.random.split(key, 40))

    def nrm(shape, scale):
        return jax.random.normal(next(ks), shape, jnp.float32) * scale

    def gain(shape):
        return 1.0 + nrm(shape, 0.02)

    L, D = DEPTH, D_MODEL
    d = {}
    d['x'] = nrm((BATCH, SEQ, D), 1.0)
    d['mem'] = nrm((BATCH, N_MEM, D), 1.0)
    d['ffn1_pre_g'] = gain((L, D))
    d['ffn1_w_gate'] = nrm((L, D, D_FF), D ** -0.5)
    d['ffn1_w_up'] = nrm((L, D, D_FF), D ** -0.5)
    d['ffn1_w_down'] = nrm((L, D_FF, D), D_FF ** -0.5)
    d['ffn1_post_g'] = gain((L, D))
    d['mix_pre_g'] = gain((L, D))
    d['w_in'] = nrm((L, D, D_IN), D ** -0.5)
    d['da_lambda_q1'] = nrm((L, DA_QK_DIM), 0.1)
    d['da_lambda_k1'] = nrm((L, DA_QK_DIM), 0.1)
    d['da_lambda_q2'] = nrm((L, DA_QK_DIM), 0.1)
    d['da_lambda_k2'] = nrm((L, DA_QK_DIM), 0.1)
    d['da_head_g'] = gain((L, DA_V_DIM))
    d['w_da_out'] = nrm((L, DA_V_WIDTH, D), DA_V_WIDTH ** -0.5)
    d['lru_conv_w'] = nrm((L, CONV_WIDTH, LRU_WIDTH), CONV_WIDTH ** -0.5)
    d['lru_conv_b'] = nrm((L, LRU_WIDTH), 0.01)
    d['lru_w_a'] = nrm((L, LRU_BLOCKS, LRU_BLOCK, LRU_BLOCK), LRU_BLOCK ** -0.5)
    d['lru_b_a'] = nrm((L, LRU_WIDTH), 0.01)
    d['lru_w_x'] = nrm((L, LRU_BLOCKS, LRU_BLOCK, LRU_BLOCK), LRU_BLOCK ** -0.5)
    d['lru_b_x'] = nrm((L, LRU_WIDTH), 0.01)
    a_max = jax.random.uniform(next(ks), (L, LRU_WIDTH), jnp.float32, 0.9, 0.999)
    s_root = a_max ** (1.0 / LRU_C)
    d['lru_lambda'] = jnp.log(s_root) - jnp.log1p(-s_root)
    d['w_lru_out'] = nrm((L, LRU_WIDTH, D), LRU_WIDTH ** -0.5)
    d['mem_g'] = gain((L, D))
    d['w_mem_kv'] = nrm((L, D, 2 * CA_WIDTH), D ** -0.5)
    d['w_ca_out'] = nrm((L, CA_WIDTH, D), CA_WIDTH ** -0.5)
    d['w_branch_gate'] = nrm((L, D, N_BRANCH * D), D ** -0.5)
    d['b_branch_gate'] = nrm((L, N_BRANCH * D), 0.01)
    d['w_mix_out'] = nrm((L, D, D), D ** -0.5)
    d['mix_post_g'] = gain((L, D))
    d['ffn2_pre_g'] = gain((L, D))
    d['ffn2_w_gate'] = nrm((L, D, D_FF), D ** -0.5)
    d['ffn2_w_up'] = nrm((L, D, D_FF), D ** -0.5)
    d['ffn2_w_down'] = nrm((L, D_FF, D), D_FF ** -0.5)
    d['ffn2_post_g'] = gain((L, D))
    return d


def reference(x, mem, ffn1_pre_g, ffn1_w_gate, ffn1_w_up, ffn1_w_down, ffn1_post_g,
              mix_pre_g, w_in, da_lambda_q1, da_lambda_k1, da_lambda_q2, da_lambda_k2,
              da_head_g, w_da_out, lru_conv_w, lru_conv_b, lru_w_a, lru_b_a, lru_w_x,
              lru_b_x, lru_lambda, w_lru_out, mem_g, w_mem_kv, w_ca_out, w_branch_gate,
              b_branch_gate, w_mix_out, mix_post_g, ffn2_pre_g, ffn2_w_gate, ffn2_w_up,
              ffn2_w_down, ffn2_post_g):
    B, S, D = x.shape
    for l in range(DEPTH):
        x = swiglu_half_step(x, ffn1_pre_g[l], ffn1_w_gate[l], ffn1_w_up[l], ffn1_w_down[l], ffn1_post_g[l])

        h = rmsnorm(x, mix_pre_g[l])
        proj = h @ w_in[l]
        q_da, k_da, v_da, x_lru, y_lru, q_ca = jnp.split(proj, IN_SPLITS, axis=-1)

        lam_init = 0.8 - 0.6 * math.exp(-0.3 * l)
        lam = (jnp.exp(jnp.sum(da_lambda_q1[l].astype(jnp.float32) * da_lambda_k1[l].astype(jnp.float32)))
               - jnp.exp(jnp.sum(da_lambda_q2[l].astype(jnp.float32) * da_lambda_k2[l].astype(jnp.float32)))
               + lam_init)
        o_da = diff_attention(q_da.reshape(B, S, DA_HEADS, 2, DA_QK_DIM),
                              k_da.reshape(B, S, DA_HEADS, 2, DA_QK_DIM),
                              v_da.reshape(B, S, DA_HEADS, DA_V_DIM),
                              lam, lam_init, da_head_g[l])
        y_da = o_da @ w_da_out[l]

        o_lru = rg_lru_branch(x_lru, y_lru, lru_conv_w[l], lru_conv_b[l], lru_w_a[l], lru_b_a[l],
                              lru_w_x[l], lru_b_x[l], lru_lambda[l])
        y_lru_out = o_lru @ w_lru_out[l]

        o_ca = memory_cross_attention(q_ca, rmsnorm(mem, mem_g[l]), w_mem_kv[l])
        y_ca = o_ca @ w_ca_out[l]

        gates = jax.nn.sigmoid(h @ w_branch_gate[l] + b_branch_gate[l]).reshape(B, S, N_BRANCH, D)
        merged = gates[:, :, 0] * y_da + gates[:, :, 1] * y_lru_out + gates[:, :, 2] * y_ca
        x = x + rmsnorm(merged @ w_mix_out[l], mix_post_g[l])

        x = swiglu_half_step(x, ffn2_pre_g[l], ffn2_w_gate[l], ffn2_w_up[l], ffn2_w_down[l], ffn2_post_g[l])
    return x
```

```python
import functools
import math

import jax
import jax.numpy as jnp
from jax import lax
from jax.experimental import pallas as pl
from jax.experimental.pallas import tpu as pltpu

D_MODEL = 1024
N_MEM = 256
DA_HEADS = 8
DA_QK_DIM = 64
DA_V_DIM = 128
LRU_BLOCKS = 8
LRU_BLOCK = 128
CONV_WIDTH = 4
LRU_C = 8.0
CA_HEADS = 4
CA_HEAD_DIM = 256
D_FF = 2816
MACARON_WEIGHT = 0.5
EPS = 1e-6
LAM_INIT = 0.8 - 0.6 * math.exp(-0.3 * 0)

SUBLANES = 8
FF_CHUNK = 256
VMEM_LIMIT = 56 * 1024 * 1024

F32 = jnp.float32
BF16 = jnp.bfloat16


def _rms(xf, g):
    return xf * lax.rsqrt(jnp.mean(xf * xf, axis=-1, keepdims=True) + EPS) * g


def _dot(a, b):
    return jnp.dot(a, b, preferred_element_type=F32)


def _dot_nt(a, b):
    return lax.dot_general(a, b, (((1,), (1,)), ((), ())), preferred_element_type=F32)


def _const_spec(shape):
    nd = len(shape)
    return pl.BlockSpec(shape, lambda *_: (0,) * nd, pipeline_mode=pl.Buffered(1))


def _params(sem):
    return pltpu.CompilerParams(dimension_semantics=sem, vmem_limit_bytes=VMEM_LIMIT)


def _ffn_kernel(x_ref, pre_g_ref, wg_ref, wu_ref, wd_ref, post_g_ref, o_ref, act_ref):
    x = x_ref[...]
    h = _rms(x, pre_g_ref[...]).astype(BF16)
    for c in range(D_FF // FF_CHUNK):
        sl = slice(c * FF_CHUNK, (c + 1) * FF_CHUNK)
        a = _dot(h, wg_ref[:, sl])
        b = _dot(h, wu_ref[:, sl])
        act_ref[:, sl] = (a * jax.nn.sigmoid(a) * b).astype(BF16)
    f = _dot(act_ref[...], wd_ref[...])
    o_ref[...] = x + MACARON_WEIGHT * _rms(f, post_g_ref[...])


def _ffn(x2d, pre_g, wg, wu, wd, post_g, tm):
    n, d = x2d.shape
    row = pl.BlockSpec((tm, d), lambda i: (i, 0))
    return pl.pallas_call(
        _ffn_kernel,
        grid=(n // tm,),
        in_specs=[row, _const_spec((1, d)), _const_spec((d, D_FF)), _const_spec((d, D_FF)),
                  _const_spec((D_FF, d)), _const_spec((1, d))],
        out_specs=row,
        out_shape=jax.ShapeDtypeStruct((n, d), F32),
        scratch_shapes=[pltpu.VMEM((tm, D_FF), BF16)],
        compiler_params=_params(("parallel",)),
        name="ffn",
    )(x2d, pre_g, wg, wu, wd, post_g)


def _proj_kernel(x_ref, g_ref, w_in_ref, w_bg_ref, b_bg_ref,
                 q_ref, k_ref, v_ref, xl_ref, yl_ref, qc_ref, gate_ref):
    d = D_MODEL
    h = _rms(x_ref[...], g_ref[...]).astype(BF16)
    outs = (q_ref, k_ref, v_ref, xl_ref, yl_ref, qc_ref)
    scales = (DA_QK_DIM ** -0.5, 1.0, 1.0, 1.0, 1.0, CA_HEAD_DIM ** -0.5)
    for j, (o_ref, sc) in enumerate(zip(outs, scales)):
        p = _dot(h, w_in_ref[:, j * d:(j + 1) * d])
        if sc != 1.0:
            p = p * sc
        o_ref[...] = p.astype(o_ref.dtype)
    for j in range(3):
        sl = slice(j * d, (j + 1) * d)
        z = _dot(h, w_bg_ref[:, sl]) + b_bg_ref[:, sl]
        gate_ref[:, sl] = jax.nn.sigmoid(z).astype(gate_ref.dtype)


def _proj(x2d, g, w_in, w_bg, b_bg, tm):
    n, d = x2d.shape
    row = pl.BlockSpec((tm, d), lambda i: (i, 0))
    bf = jax.ShapeDtypeStruct((n, d), BF16)
    f32 = jax.ShapeDtypeStruct((n, d), F32)
    return pl.pallas_call(
        _proj_kernel,
        grid=(n // tm,),
        in_specs=[row, _const_spec((1, d)), _const_spec(w_in.shape), _const_spec(w_bg.shape),
                  _const_spec((1, 3 * d))],
        out_specs=[row, row, row, row, row, row, pl.BlockSpec((tm, 3 * d), lambda i: (i, 0))],
        out_shape=[bf, bf, bf, f32, f32, bf, jax.ShapeDtypeStruct((n, 3 * d), BF16)],
        compiler_params=_params(("parallel",)),
        name="proj",
    )(x2d, g, w_in, w_bg, b_bg)


def _diff_attn_kernel(slopes_ref, q_ref, k_ref, v_ref, lq1_ref, lk1_ref, lq2_ref, lk2_ref, hg_ref,
                      o_ref, *, tq):
    hh = pl.program_id(1)
    qi = pl.program_id(2)
    slope = slopes_ref[hh]
    q = q_ref[0]
    lane = lax.broadcasted_iota(jnp.int32, (1, 2 * DA_QK_DIM), 1)
    zero = jnp.zeros_like(q)
    qq = jnp.concatenate([jnp.where(lane < DA_QK_DIM, q, zero),
                          jnp.where(lane >= DA_QK_DIM, q, zero)], axis=0)
    col = lax.broadcasted_iota(jnp.int32, (1, tq), 1)

    def scores(j):
        start = pl.multiple_of(j * tq, tq)
        kb = k_ref[0, pl.ds(start, tq), :]
        vb = v_ref[0, pl.ds(start, tq), :]
        s = _dot_nt(qq, kb)
        bias = slope * (col + (j - qi) * tq).astype(F32)
        return s + bias, vb

    def update(carry, s, vb):
        m, l, acc = carry
        m_new = jnp.maximum(m, jnp.max(s, axis=-1, keepdims=True))
        alpha = jnp.exp(m - m_new)
        p = jnp.exp(s - m_new)
        l = alpha * l + jnp.sum(p, axis=-1, keepdims=True)
        acc = alpha * acc + _dot(p.astype(BF16), vb)
        return m_new, l, acc

    def body(j, carry):
        s, vb = scores(j)
        return update(carry, s, vb)

    init = (jnp.full((2 * tq, 1), -jnp.inf, F32), jnp.zeros((2 * tq, 1), F32),
            jnp.zeros((2 * tq, DA_V_DIM), F32))
    carry = lax.fori_loop(0, qi, body, init)
    s, vb = scores(qi)
    r = lax.broadcasted_iota(jnp.int32, (tq, tq), 0)
    c = lax.broadcasted_iota(jnp.int32, (tq, tq), 1)
    keep = jnp.concatenate([r >= c, r >= c], axis=0)
    m, l, acc = update(carry, jnp.where(keep, s, -jnp.inf), vb)

    lam = (jnp.exp(jnp.sum(lq1_ref[...] * lk1_ref[...], axis=-1, keepdims=True))
           - jnp.exp(jnp.sum(lq2_ref[...] * lk2_ref[...], axis=-1, keepdims=True)) + LAM_INIT)
    o = acc / l
    o = o[:tq] - lam * o[tq:]
    o_ref[0] = (_rms(o, hg_ref[...]) * (1.0 - LAM_INIT)).astype(o_ref.dtype)


def _diff_attn(slopes, q, k, v, lq1, lk1, lq2, lk2, head_g, tq):
    b, s, d = q.shape
    qspec = pl.BlockSpec((1, tq, DA_V_DIM), lambda bi, hi, qi: (bi, qi, hi))
    kvspec = pl.BlockSpec((1, s, DA_V_DIM), lambda bi, hi, qi: (bi, 0, hi))
    vec = lambda w: pl.BlockSpec((1, w), lambda bi, hi, qi: (0, 0))
    return pl.pallas_call(
        functools.partial(_diff_attn_kernel, tq=tq),
        grid=(b, DA_HEADS, s // tq),
        in_specs=[pl.BlockSpec(memory_space=pltpu.SMEM), qspec, kvspec, kvspec,
                  vec(DA_QK_DIM), vec(DA_QK_DIM), vec(DA_QK_DIM), vec(DA_QK_DIM), vec(DA_V_DIM)],
        out_specs=qspec,
        out_shape=jax.ShapeDtypeStruct((b, s, d), BF16),
        compiler_params=_params(("parallel", "parallel", "arbitrary")),
        name="diff_attn",
    )(slopes, q, k, v, lq1, lk1, lq2, lk2, head_g)


def _lru_kernel(x_ref, y_ref, cw_ref, cb_ref, wax_ref, ba_ref, bx_ref, lam_ref, o_ref,
                xbuf_ref, a_ref, u_ref, h_ref, carry_ref, *, tm):
    si = pl.program_id(1)
    w = x_ref.shape[-1]

    @pl.when(si == 0)
    def _():
        xbuf_ref[0:SUBLANES, :] = jnp.zeros((SUBLANES, w), F32)
        carry_ref[...] = jnp.zeros((SUBLANES, w), F32)

    x = x_ref[0]
    xbuf_ref[SUBLANES:, :] = x
    xc = cb_ref[...] + x * cw_ref[CONV_WIDTH - 1:CONV_WIDTH, :]
    for back in range(1, CONV_WIDTH):
        t = CONV_WIDTH - 1 - back
        xc = xc + xbuf_ref[SUBLANES - back:SUBLANES - back + tm, :] * cw_ref[t:t + 1, :]
    xbuf_ref[0:SUBLANES, :] = x[tm - SUBLANES:, :]

    nlam = -lam_ref[...]
    softplus = jnp.maximum(nlam, 0.0) + jnp.log1p(jnp.exp(-jnp.abs(nlam)))
    xcb = xc.astype(BF16)
    for n in range(LRU_BLOCKS):
        sl = slice(n * LRU_BLOCK, (n + 1) * LRU_BLOCK)
        z = _dot(xcb[:, sl], wax_ref[n])
        r = jax.nn.sigmoid(z[:, :LRU_BLOCK] + ba_ref[:, sl])
        i = jax.nn.sigmoid(z[:, LRU_BLOCK:] + bx_ref[:, sl])
        log_a = -LRU_C * r * softplus[:, sl]
        a = jnp.exp(log_a)
        a_ref[:, sl] = a
        u_ref[:, sl] = jnp.sqrt(1.0 - a * a) * (i * xc[:, sl])

    a = a_ref[...]
    u = u_ref[...]
    row = lax.broadcasted_iota(jnp.int32, (tm, w), 0) & (SUBLANES - 1)
    for dist in (1, 2, 4):
        a_prev = pltpu.roll(a, dist, axis=0)
        u_prev = pltpu.roll(u, dist, axis=0)
        ok = row >= dist
        u = jnp.where(ok, a * u_prev + u, u)
        a = jnp.where(ok, a * a_prev, a)
    a_ref[...] = a
    u_ref[...] = u

    def group(g, carry):
        rows = pl.ds(pl.multiple_of(g * SUBLANES, SUBLANES), SUBLANES)
        hg = a_ref[rows, :] * carry + u_ref[rows, :]
        h_ref[rows, :] = hg
        return jnp.broadcast_to(hg[SUBLANES - 1:SUBLANES, :], (SUBLANES, w))

    carry_ref[...] = lax.fori_loop(0, tm // SUBLANES, group, carry_ref[...])
    o_ref[0] = (h_ref[...] * jax.nn.gelu(y_ref[0])).astype(o_ref.dtype)


def _lru(x_lru, y_lru, conv_w, conv_b, wax, b_a, b_x, lam, tm):
    b, s, w = x_lru.shape
    row = pl.BlockSpec((1, tm, w), lambda bi, si: (bi, si, 0))
    vec = pl.BlockSpec((1, w), lambda bi, si: (0, 0))
    return pl.pallas_call(
        functools.partial(_lru_kernel, tm=tm),
        grid=(b, s // tm),
        in_specs=[row, row, pl.BlockSpec((CONV_WIDTH, w), lambda bi, si: (0, 0)), vec,
                  pl.BlockSpec(wax.shape, lambda bi, si: (0, 0, 0)), vec, vec, vec],
        out_specs=row,
        out_shape=jax.ShapeDtypeStruct((b, s, w), BF16),
        scratch_shapes=[pltpu.VMEM((tm + SUBLANES, w), F32), pltpu.VMEM((tm, w), F32),
                        pltpu.VMEM((tm, w), F32), pltpu.VMEM((tm, w), F32),
                        pltpu.VMEM((SUBLANES, w), F32)],
        compiler_params=_params(("parallel", "arbitrary")),
        name="lru",
    )(x_lru, y_lru, conv_w, conv_b, wax, b_a, b_x, lam)


def _mem_kv_kernel(mem_ref, g_ref, w_ref, km_ref, vm_ref):
    mn = _rms(mem_ref[0], g_ref[...]).astype(BF16)
    d = D_MODEL
    km_ref[0] = _dot(mn, w_ref[:, :d]).astype(km_ref.dtype)
    vm_ref[0] = _dot(mn, w_ref[:, d:]).astype(vm_ref.dtype)


def _mem_kv(mem, g, w):
    b, m, d = mem.shape
    blk = pl.BlockSpec((1, m, d), lambda bi: (bi, 0, 0))
    out = jax.ShapeDtypeStruct((b, m, d), BF16)
    return pl.pallas_call(
        _mem_kv_kernel,
        grid=(b,),
        in_specs=[blk, _const_spec((1, d)), _const_spec(w.shape)],
        out_specs=[blk, blk],
        out_shape=[out, out],
        compiler_params=_params(("parallel",)),
        name="mem_kv",
    )(mem, g, w)


def _merge_kernel(x_ref, oda_ref, olru_ref, qc_ref, km_ref, vm_ref, gate_ref,
                  wda_ref, wlru_ref, wca_ref, wmix_ref, g_ref, o_ref, oca_ref):
    d = D_MODEL
    qc = qc_ref[0]
    for hh in range(CA_HEADS):
        sl = slice(hh * CA_HEAD_DIM, (hh + 1) * CA_HEAD_DIM)
        s = _dot_nt(qc[:, sl], km_ref[0, :, sl])
        p = jnp.exp(s - jnp.max(s, axis=-1, keepdims=True))
        o = _dot(p.astype(BF16), vm_ref[0, :, sl]) / jnp.sum(p, axis=-1, keepdims=True)
        oca_ref[:, sl] = o.astype(BF16)
    merged = (gate_ref[0, :, 0:d].astype(F32) * _dot(oda_ref[0], wda_ref[...])
              + gate_ref[0, :, d:2 * d].astype(F32) * _dot(olru_ref[0], wlru_ref[...])
              + gate_ref[0, :, 2 * d:].astype(F32) * _dot(oca_ref[...], wca_ref[...]))
    out = _dot(merged.astype(BF16), wmix_ref[...])
    o_ref[0] = x_ref[0] + _rms(out, g_ref[...])


def _merge(x, o_da, o_lru, q_ca, km, vm, gates, w_da, w_lru, w_ca, w_mix, g, tm):
    b, s, d = x.shape
    row = pl.BlockSpec((1, tm, d), lambda bi, si: (bi, si, 0))
    memspec = pl.BlockSpec((1, N_MEM, d), lambda bi, si: (bi, 0, 0))
    wspec = _const_spec((d, d))
    return pl.pallas_call(
        _merge_kernel,
        grid=(b, s // tm),
        in_specs=[row, row, row, row, memspec, memspec,
                  pl.BlockSpec((1, tm, 3 * d), lambda bi, si: (bi, si, 0)),
                  wspec, wspec, wspec, wspec, _const_spec((1, d))],
        out_specs=row,
        out_shape=jax.ShapeDtypeStruct((b, s, d), F32),
        scratch_shapes=[pltpu.VMEM((tm, d), BF16)],
        compiler_params=_params(("parallel", "parallel")),
        name="merge",
    )(x, o_da, o_lru, q_ca, km, vm, gates, w_da, w_lru, w_ca, w_mix, g)


def kernel(x, mem, ffn1_pre_g, ffn1_w_gate, ffn1_w_up, ffn1_w_down, ffn1_post_g, mix_pre_g, w_in, da_lambda_q1, da_lambda_k1, da_lambda_q2, da_lambda_k2, da_head_g, w_da_out, lru_conv_w, lru_conv_b, lru_w_a, lru_b_a, lru_w_x, lru_b_x, lru_lambda, w_lru_out, mem_g, w_mem_kv, w_ca_out, w_branch_gate, b_branch_gate, w_mix_out, mix_post_g, ffn2_pre_g, ffn2_w_gate, ffn2_w_up, ffn2_w_down, ffn2_post_g):
    b, s, d = x.shape
    n = b * s
    bf = lambda w: w.astype(BF16)
    slopes = jnp.exp2(-8.0 * jnp.arange(1, DA_HEADS + 1, dtype=F32) / DA_HEADS)

    x2d = x.reshape(n, d)
    for l in range(ffn1_pre_g.shape[0]):
        x2d = _ffn(x2d, ffn1_pre_g[l][None], bf(ffn1_w_gate[l]), bf(ffn1_w_up[l]), bf(ffn1_w_down[l]),
                   ffn1_post_g[l][None], tm=512)
        q_da, k_da, v_da, x_lru, y_lru, q_ca, gates = _proj(
            x2d, mix_pre_g[l][None], bf(w_in[l]), bf(w_branch_gate[l]), b_branch_gate[l][None], tm=512)
        r3 = lambda t: t.reshape(b, s, -1)
        o_da = _diff_attn(slopes, r3(q_da), r3(k_da), r3(v_da), da_lambda_q1[l][None], da_lambda_k1[l][None],
                          da_lambda_q2[l][None], da_lambda_k2[l][None], da_head_g[l][None], tq=512)
        wax = bf(jnp.concatenate([lru_w_a[l], lru_w_x[l]], axis=-1))
        o_lru = _lru(r3(x_lru), r3(y_lru), lru_conv_w[l], lru_conv_b[l][None], wax, lru_b_a[l][None],
                     lru_b_x[l][None], lru_lambda[l][None], tm=512)
        km, vm = _mem_kv(mem, mem_g[l][None], bf(w_mem_kv[l]))
        x3 = _merge(r3(x2d), o_da, o_lru, r3(q_ca), km, vm, r3(gates), bf(w_da_out[l]), bf(w_lru_out[l]),
                    bf(w_ca_out[l]), bf(w_mix_out[l]), mix_post_g[l][None], tm=512)
        x2d = _ffn(x3.reshape(n, d), ffn2_pre_g[l][None], bf(ffn2_w_gate[l]), bf(ffn2_w_up[l]),
                   bf(ffn2_w_down[l]), ffn2_post_g[l][None], tm=512)
    return x2d.reshape(b, s, d)
```

```python
import functools
import math

import jax
import jax.numpy as jnp
from jax import lax
from jax.experimental import pallas as pl
from jax.experimental.pallas import tpu as pltpu

D_MODEL = 1024
N_MEM = 256
DA_HEADS = 8
DA_QK_DIM = 64
DA_V_DIM = 128
LRU_BLOCKS = 8
LRU_BLOCK = 128
CONV_WIDTH = 4
LRU_C = 8.0
CA_HEADS = 4
CA_HEAD_DIM = 256
D_FF = 2816
MACARON_WEIGHT = 0.5
EPS = 1e-6
LAM_INIT = 0.8 - 0.6 * math.exp(-0.3 * 0)
LOG2E = math.log2(math.e)

SUBLANES = 8
LANES = 128
ROW_GROUP = 16
FF_CHUNK = 256
VMEM_LIMIT = 56 * 1024 * 1024

F32 = jnp.float32
BF16 = jnp.bfloat16


def _rms(xf, g):
    return xf * lax.rsqrt(jnp.mean(xf * xf, axis=-1, keepdims=True) + EPS) * g


def _dot(a, b):
    return jnp.dot(a, b, preferred_element_type=F32)


def _dot_nt(a, b):
    return lax.dot_general(a, b, (((1,), (1,)), ((), ())), preferred_element_type=F32)


def _const_spec(shape):
    nd = len(shape)
    return pl.BlockSpec(shape, lambda *_: (0,) * nd, pipeline_mode=pl.Buffered(1))


def _params(sem):
    return pltpu.CompilerParams(dimension_semantics=sem, vmem_limit_bytes=VMEM_LIMIT)


def _ffn_kernel(x_ref, pre_g_ref, wg_ref, wu_ref, wd_ref, post_g_ref, o_ref, act_ref):
    x = x_ref[...]
    h = _rms(x, pre_g_ref[...]).astype(BF16)
    for c in range(D_FF // FF_CHUNK):
        sl = slice(c * FF_CHUNK, (c + 1) * FF_CHUNK)
        a = _dot(h, wg_ref[:, sl])
        b = _dot(h, wu_ref[:, sl])
        act_ref[:, sl] = (a * jax.nn.sigmoid(a) * b).astype(BF16)
    f = _dot(act_ref[...], wd_ref[...])
    o_ref[...] = x + MACARON_WEIGHT * _rms(f, post_g_ref[...])


def _ffn(x2d, pre_g, wg, wu, wd, post_g, tm):
    n, d = x2d.shape
    row = pl.BlockSpec((tm, d), lambda i: (i, 0))
    return pl.pallas_call(
        _ffn_kernel,
        grid=(n // tm,),
        in_specs=[row, _const_spec((1, d)), _const_spec((d, D_FF)), _const_spec((d, D_FF)),
                  _const_spec((D_FF, d)), _const_spec((1, d))],
        out_specs=row,
        out_shape=jax.ShapeDtypeStruct((n, d), F32),
        scratch_shapes=[pltpu.VMEM((tm, D_FF), BF16)],
        compiler_params=_params(("parallel",)),
        name="ffn",
    )(x2d, pre_g, wg, wu, wd, post_g)


def _proj_kernel(x_ref, g_ref, w_in_ref, w_bg_ref, b_bg_ref,
                 q_ref, k_ref, v_ref, xl_ref, yl_ref, qc_ref, gate_ref):
    d = D_MODEL
    h = _rms(x_ref[...], g_ref[...]).astype(BF16)
    outs = (q_ref, k_ref, v_ref, xl_ref, yl_ref, qc_ref)
    scales = (DA_QK_DIM ** -0.5 * LOG2E, 1.0, 1.0, 1.0, 1.0, CA_HEAD_DIM ** -0.5)
    for j, (o_ref, sc) in enumerate(zip(outs, scales)):
        p = _dot(h, w_in_ref[:, j * d:(j + 1) * d])
        if sc != 1.0:
            p = p * sc
        o_ref[...] = p.astype(o_ref.dtype)
    for j in range(3):
        sl = slice(j * d, (j + 1) * d)
        z = _dot(h, w_bg_ref[:, sl]) + b_bg_ref[:, sl]
        gate_ref[:, sl] = jax.nn.sigmoid(z).astype(gate_ref.dtype)


def _proj(x2d, g, w_in, w_bg, b_bg, tm):
    n, d = x2d.shape
    row = pl.BlockSpec((tm, d), lambda i: (i, 0))
    bf = jax.ShapeDtypeStruct((n, d), BF16)
    f32 = jax.ShapeDtypeStruct((n, d), F32)
    return pl.pallas_call(
        _proj_kernel,
        grid=(n // tm,),
        in_specs=[row, _const_spec((1, d)), _const_spec(w_in.shape), _const_spec(w_bg.shape),
                  _const_spec((1, 3 * d))],
        out_specs=[row, row, row, row, row, row, pl.BlockSpec((tm, 3 * d), lambda i: (i, 0))],
        out_shape=[bf, bf, bf, f32, f32, bf, jax.ShapeDtypeStruct((n, 3 * d), BF16)],
        compiler_params=_params(("parallel",)),
        name="proj",
    )(x2d, g, w_in, w_bg, b_bg)


def _diff_attn_kernel(slopes_ref, zero_ref, q_ref, k_ref, v_ref, lq1_ref, lk1_ref, lq2_ref, lk2_ref, hg_ref, o_ref,
                      vaug_ref, qm_ref, m_ref, acc_ref, s0_ref, s1_ref, p0_ref, p1_ref, a0_ref, a1_ref, *, tq):
    seq = q_ref.shape[1]
    dv = DA_V_DIM
    slope = slopes_ref[pl.program_id(1)] * LOG2E
    n_col_tiles = tq // LANES
    s_refs, p_refs, alpha_refs = (s0_ref, s1_ref), (p0_ref, p1_ref), (a0_ref, a1_ref)
    dyn0 = zero_ref[0]

    vaug_ref[:, :dv] = v_ref[0]
    vaug_ref[:, dv:] = jnp.ones((seq, dv), BF16)
    lam = (jnp.exp(jnp.sum(lq1_ref[...] * lk1_ref[...], axis=-1, keepdims=True))
           - jnp.exp(jnp.sum(lq2_ref[...] * lk2_ref[...], axis=-1, keepdims=True)) + LAM_INIT)

    lane = lax.broadcasted_iota(jnp.int32, (1, 2 * DA_QK_DIM), 1)
    col = lax.broadcasted_iota(jnp.int32, (1, tq), 1)
    rr = lax.broadcasted_iota(jnp.int32, (ROW_GROUP, LANES), 0)
    cc = lax.broadcasted_iota(jnp.int32, (ROW_GROUP, LANES), 1)
    pieces = [(qi, j) for qi in range(seq // tq) for j in range(qi + 1)]

    def load_queries(qi):
        q = q_ref[0, qi * tq:(qi + 1) * tq, :]
        zero = jnp.zeros_like(q)
        qm_ref[qi % 2, :tq, :] = jnp.where(lane < DA_QK_DIM, q, zero)
        qm_ref[qi % 2, tq:, :] = jnp.where(lane >= DA_QK_DIM, q, zero)

    def start_scores(n):
        qi, j = pieces[n]
        s_refs[n % 2][...] = _dot_nt(qm_ref[qi % 2], k_ref[0, j * tq:(j + 1) * tq, :])

    def softmax_pv(n):
        qi, j = pieces[n]
        buf = n % 2
        first, diagonal = j == 0, j == qi
        bias = slope * (col + (j - qi) * tq).astype(F32)
        for g in range(2 * tq // ROW_GROUP):
            r0 = g * ROW_GROUP
            rq = r0 % tq
            rows = pl.ds(pl.multiple_of(r0 + dyn0, ROW_GROUP), ROW_GROUP)
            tiles = []
            for c in range(n_col_tiles):
                c0 = c * LANES
                t = s_refs[buf][rows, c0:c0 + LANES] + bias[:, c0:c0 + LANES]
                if diagonal and c0 + LANES - 1 > rq:
                    t = jnp.where(rr + rq >= cc + c0, t, -jnp.inf)
                tiles.append(t)
            mx = jnp.max(functools.reduce(jnp.maximum, tiles), axis=-1, keepdims=True)
            if first:
                m_new = jnp.broadcast_to(mx, (ROW_GROUP, LANES))
            else:
                m_old = m_ref[rows, :]
                m_new = jnp.maximum(m_old, mx)
                alpha_refs[buf][rows, :] = jnp.exp2(m_old - m_new)
            m_ref[rows, :] = m_new
            for c in range(n_col_tiles):
                c0 = c * LANES
                p_refs[buf][rows, c0:c0 + LANES] = jnp.exp2(tiles[c] - m_new).astype(BF16)
        pv = _dot(p_refs[buf][...], vaug_ref[j * tq:(j + 1) * tq, :])
        if first:
            acc_ref[...] = pv
        else:
            alpha = alpha_refs[buf][...]
            acc_ref[:, :dv] = alpha * acc_ref[:, :dv] + pv[:, :dv]
            acc_ref[:, dv:] = alpha * acc_ref[:, dv:] + pv[:, dv:]

    def finalize(qi):
        o = (acc_ref[:tq, :dv] / acc_ref[:tq, dv:]) - lam * (acc_ref[tq:, :dv] / acc_ref[tq:, dv:])
        o_ref[0, qi * tq:(qi + 1) * tq, :] = (_rms(o, hg_ref[...]) * (1.0 - LAM_INIT)).astype(o_ref.dtype)

    load_queries(0)
    start_scores(0)
    for n, (qi, j) in enumerate(pieces):
        if n + 1 < len(pieces):
            if pieces[n + 1][0] != qi:
                load_queries(pieces[n + 1][0])
            start_scores(n + 1)
        softmax_pv(n)
        if j == qi:
            finalize(qi)


def _diff_attn(slopes, q, k, v, lq1, lk1, lq2, lk2, head_g, tq):
    b, s, d = q.shape
    dv = DA_V_DIM
    head = pl.BlockSpec((1, s, dv), lambda bi, hi: (bi, 0, hi))
    vec = lambda w: pl.BlockSpec((1, w), lambda bi, hi: (0, 0))
    return pl.pallas_call(
        functools.partial(_diff_attn_kernel, tq=tq),
        grid=(b, DA_HEADS),
        in_specs=[pl.BlockSpec(memory_space=pltpu.SMEM), pl.BlockSpec(memory_space=pltpu.SMEM), head, head, head,
                  vec(DA_QK_DIM), vec(DA_QK_DIM), vec(DA_QK_DIM), vec(DA_QK_DIM), vec(dv)],
        out_specs=head,
        out_shape=jax.ShapeDtypeStruct((b, s, d), BF16),
        scratch_shapes=[pltpu.VMEM((s, 2 * dv), BF16),
                        pltpu.VMEM((2, 2 * tq, dv), BF16),
                        pltpu.VMEM((2 * tq, LANES), F32),
                        pltpu.VMEM((2 * tq, 2 * dv), F32),
                        pltpu.VMEM((2 * tq, tq), F32), pltpu.VMEM((2 * tq, tq), F32),
                        pltpu.VMEM((2 * tq, tq), BF16), pltpu.VMEM((2 * tq, tq), BF16),
                        pltpu.VMEM((2 * tq, LANES), F32), pltpu.VMEM((2 * tq, LANES), F32)],
        compiler_params=_params(("parallel", "parallel")),
        name="diff_attn",
    )(slopes, jnp.zeros((1,), jnp.int32), q, k, v, lq1, lk1, lq2, lk2, head_g)


def _lru_kernel(x_ref, y_ref, cw_ref, cb_ref, wax_ref, ba_ref, bx_ref, lam_ref, o_ref,
                xbuf_ref, a_ref, u_ref, h_ref, carry_ref, *, tm):
    si = pl.program_id(1)
    w = x_ref.shape[-1]

    @pl.when(si == 0)
    def _():
        xbuf_ref[0:SUBLANES, :] = jnp.zeros((SUBLANES, w), F32)
        carry_ref[...] = jnp.zeros((SUBLANES, w), F32)

    x = x_ref[0]
    xbuf_ref[SUBLANES:, :] = x
    xc = cb_ref[...] + x * cw_ref[CONV_WIDTH - 1:CONV_WIDTH, :]
    for back in range(1, CONV_WIDTH):
        t = CONV_WIDTH - 1 - back
        xc = xc + xbuf_ref[SUBLANES - back:SUBLANES - back + tm, :] * cw_ref[t:t + 1, :]
    xbuf_ref[0:SUBLANES, :] = x[tm - SUBLANES:, :]

    nlam = -lam_ref[...]
    softplus = jnp.maximum(nlam, 0.0) + jnp.log1p(jnp.exp(-jnp.abs(nlam)))
    xcb = xc.astype(BF16)
    for n in range(LRU_BLOCKS):
        sl = slice(n * LRU_BLOCK, (n + 1) * LRU_BLOCK)
        z = _dot(xcb[:, sl], wax_ref[n])
        r = jax.nn.sigmoid(z[:, :LRU_BLOCK] + ba_ref[:, sl])
        i = jax.nn.sigmoid(z[:, LRU_BLOCK:] + bx_ref[:, sl])
        a = jnp.exp(-LRU_C * r * softplus[:, sl])
        a_ref[:, sl] = a
        gap = 1.0 - a * a
        root = jnp.where(gap > 0.0, gap * lax.rsqrt(gap), 0.0)
        u_ref[:, sl] = root * (i * xc[:, sl])

    row = lax.broadcasted_iota(jnp.int32, (SUBLANES, w), 0)

    def group(g, carry):
        rows = pl.ds(pl.multiple_of(g * SUBLANES, SUBLANES), SUBLANES)
        a = a_ref[rows, :]
        u = u_ref[rows, :]
        for dist in (1, 2, 4):
            ok = row >= dist
            a_prev = jnp.where(ok, pltpu.roll(a, dist, axis=0), 1.0)
            u_prev = jnp.where(ok, pltpu.roll(u, dist, axis=0), 0.0)
            u = a * u_prev + u
            a = a * a_prev
        hg = a * carry + u
        h_ref[rows, :] = hg
        return jnp.broadcast_to(hg[SUBLANES - 1:SUBLANES, :], (SUBLANES, w))

    carry_ref[...] = lax.fori_loop(0, tm // SUBLANES, group, carry_ref[...], unroll=4)
    o_ref[0] = (h_ref[...] * jax.nn.gelu(y_ref[0])).astype(o_ref.dtype)


def _lru(x_lru, y_lru, conv_w, conv_b, wax, b_a, b_x, lam, tm):
    b, s, w = x_lru.shape
    row = pl.BlockSpec((1, tm, w), lambda bi, si: (bi, si, 0))
    vec = pl.BlockSpec((1, w), lambda bi, si: (0, 0))
    return pl.pallas_call(
        functools.partial(_lru_kernel, tm=tm),
        grid=(b, s // tm),
        in_specs=[row, row, pl.BlockSpec((CONV_WIDTH, w), lambda bi, si: (0, 0)), vec,
                  pl.BlockSpec(wax.shape, lambda bi, si: (0, 0, 0)), vec, vec, vec],
        out_specs=row,
        out_shape=jax.ShapeDtypeStruct((b, s, w), BF16),
        scratch_shapes=[pltpu.VMEM((tm + SUBLANES, w), F32), pltpu.VMEM((tm, w), F32),
                        pltpu.VMEM((tm, w), F32), pltpu.VMEM((tm, w), F32),
                        pltpu.VMEM((SUBLANES, w), F32)],
        compiler_params=_params(("parallel", "arbitrary")),
        name="lru",
    )(x_lru, y_lru, conv_w, conv_b, wax, b_a, b_x, lam)


def _mem_kv_kernel(mem_ref, g_ref, w_ref, km_ref, vm_ref):
    mn = _rms(mem_ref[0], g_ref[...]).astype(BF16)
    d = D_MODEL
    km_ref[0] = _dot(mn, w_ref[:, :d]).astype(km_ref.dtype)
    vm_ref[0] = _dot(mn, w_ref[:, d:]).astype(vm_ref.dtype)


def _mem_kv(mem, g, w):
    b, m, d = mem.shape
    blk = pl.BlockSpec((1, m, d), lambda bi: (bi, 0, 0))
    out = jax.ShapeDtypeStruct((b, m, d), BF16)
    return pl.pallas_call(
        _mem_kv_kernel,
        grid=(b,),
        in_specs=[blk, _const_spec((1, d)), _const_spec(w.shape)],
        out_specs=[blk, blk],
        out_shape=[out, out],
        compiler_params=_params(("parallel",)),
        name="mem_kv",
    )(mem, g, w)


def _merge_kernel(x_ref, oda_ref, olru_ref, qc_ref, km_ref, vm_ref, gate_ref,
                  wda_ref, wlru_ref, wca_ref, wmix_ref, g_ref, o_ref, oca_ref):
    d = D_MODEL
    qc = qc_ref[0]
    for hh in range(CA_HEADS):
        sl = slice(hh * CA_HEAD_DIM, (hh + 1) * CA_HEAD_DIM)
        s = _dot_nt(qc[:, sl], km_ref[0, :, sl])
        p = jnp.exp(s - jnp.max(s, axis=-1, keepdims=True))
        o = _dot(p.astype(BF16), vm_ref[0, :, sl]) / jnp.sum(p, axis=-1, keepdims=True)
        oca_ref[:, sl] = o.astype(BF16)
    merged = (gate_ref[0, :, 0:d].astype(F32) * _dot(oda_ref[0], wda_ref[...])
              + gate_ref[0, :, d:2 * d].astype(F32) * _dot(olru_ref[0], wlru_ref[...])
              + gate_ref[0, :, 2 * d:].astype(F32) * _dot(oca_ref[...], wca_ref[...]))
    out = _dot(merged.astype(BF16), wmix_ref[...])
    o_ref[0] = x_ref[0] + _rms(out, g_ref[...])


def _merge(x, o_da, o_lru, q_ca, km, vm, gates, w_da, w_lru, w_ca, w_mix, g, tm):
    b, s, d = x.shape
    row = pl.BlockSpec((1, tm, d), lambda bi, si: (bi, si, 0))
    memspec = pl.BlockSpec((1, N_MEM, d), lambda bi, si: (bi, 0, 0))
    wspec = _const_spec((d, d))
    return pl.pallas_call(
        _merge_kernel,
        grid=(b, s // tm),
        in_specs=[row, row, row, row, memspec, memspec,
                  pl.BlockSpec((1, tm, 3 * d), lambda bi, si: (bi, si, 0)),
                  wspec, wspec, wspec, wspec, _const_spec((1, d))],
        out_specs=row,
        out_shape=jax.ShapeDtypeStruct((b, s, d), F32),
        scratch_shapes=[pltpu.VMEM((tm, d), BF16)],
        compiler_params=_params(("parallel", "parallel")),
        name="merge",
    )(x, o_da, o_lru, q_ca, km, vm, gates, w_da, w_lru, w_ca, w_mix, g)


def kernel(x, mem, ffn1_pre_g, ffn1_w_gate, ffn1_w_up, ffn1_w_down, ffn1_post_g, mix_pre_g, w_in, da_lambda_q1, da_lambda_k1, da_lambda_q2, da_lambda_k2, da_head_g, w_da_out, lru_conv_w, lru_conv_b, lru_w_a, lru_b_a, lru_w_x, lru_b_x, lru_lambda, w_lru_out, mem_g, w_mem_kv, w_ca_out, w_branch_gate, b_branch_gate, w_mix_out, mix_post_g, ffn2_pre_g, ffn2_w_gate, ffn2_w_up, ffn2_w_down, ffn2_post_g):
    b, s, d = x.shape
    n = b * s
    bf = lambda w: w.astype(BF16)
    slopes = jnp.exp2(-8.0 * jnp.arange(1, DA_HEADS + 1, dtype=F32) / DA_HEADS)

    x2d = x.reshape(n, d)
    for l in range(ffn1_pre_g.shape[0]):
        x2d = _ffn(x2d, ffn1_pre_g[l][None], bf(ffn1_w_gate[l]), bf(ffn1_w_up[l]), bf(ffn1_w_down[l]),
                   ffn1_post_g[l][None], tm=512)
        q_da, k_da, v_da, x_lru, y_lru, q_ca, gates = _proj(
            x2d, mix_pre_g[l][None], bf(w_in[l]), bf(w_branch_gate[l]), b_branch_gate[l][None], tm=512)
        r3 = lambda t: t.reshape(b, s, -1)
        o_da = _diff_attn(slopes, r3(q_da), r3(k_da), r3(v_da), da_lambda_q1[l][None], da_lambda_k1[l][None],
                          da_lambda_q2[l][None], da_lambda_k2[l][None], da_head_g[l][None], tq=512)
        wax = bf(jnp.concatenate([lru_w_a[l], lru_w_x[l]], axis=-1))
        o_lru = _lru(r3(x_lru), r3(y_lru), lru_conv_w[l], lru_conv_b[l][None], wax, lru_b_a[l][None],
                     lru_b_x[l][None], lru_lambda[l][None], tm=512)
        km, vm = _mem_kv(mem, mem_g[l][None], bf(w_mem_kv[l]))
        x3 = _merge(r3(x2d), o_da, o_lru, r3(q_ca), km, vm, r3(gates), bf(w_da_out[l]), bf(w_lru_out[l]),
                    bf(w_ca_out[l]), bf(w_mix_out[l]), mix_post_g[l][None], tm=512)
        x2d = _ffn(x3.reshape(n, d), ffn2_pre_g[l][None], bf(ffn2_w_gate[l]), bf(ffn2_w_up[l]),
                   bf(ffn2_w_down[l]), ffn2_post_g[l][None], tm=512)
    return x2d.reshape(b, s, d)
```

```python
import functools
import math

import jax
import jax.numpy as jnp
from jax import lax
from jax.experimental import pallas as pl
from jax.experimental.pallas import tpu as pltpu

D_MODEL = 1024
N_MEM = 256
DA_HEADS = 8
DA_QK_DIM = 64
DA_V_DIM = 128
LRU_BLOCKS = 8
LRU_BLOCK = 128
CONV_WIDTH = 4
LRU_C = 8.0
CA_HEADS = 4
CA_HEAD_DIM = 256
D_FF = 2816
MACARON_WEIGHT = 0.5
EPS = 1e-6
LAM_INIT = 0.8 - 0.6 * math.exp(-0.3 * 0)
LOG2E = math.log2(math.e)

SUBLANES = 8
LANES = 128
ROW_GROUP = 16
FF_CHUNK = 256
VMEM_LIMIT = 56 * 1024 * 1024

F32 = jnp.float32
BF16 = jnp.bfloat16


def _rms(xf, g):
    return xf * lax.rsqrt(jnp.mean(xf * xf, axis=-1, keepdims=True) + EPS) * g


def _dot(a, b):
    return jnp.dot(a, b, preferred_element_type=F32)


def _dot_nt(a, b):
    return lax.dot_general(a, b, (((1,), (1,)), ((), ())), preferred_element_type=F32)


def _const_spec(shape):
    nd = len(shape)
    return pl.BlockSpec(shape, lambda *_: (0,) * nd, pipeline_mode=pl.Buffered(1))


def _params(sem):
    return pltpu.CompilerParams(dimension_semantics=sem, vmem_limit_bytes=VMEM_LIMIT)


def _ffn_kernel(x_ref, pre_g_ref, wg_ref, wu_ref, wd_ref, post_g_ref, o_ref, act_ref):
    x = x_ref[...]
    h = _rms(x, pre_g_ref[...]).astype(BF16)
    for c in range(D_FF // FF_CHUNK):
        sl = slice(c * FF_CHUNK, (c + 1) * FF_CHUNK)
        a = _dot(h, wg_ref[:, sl])
        b = _dot(h, wu_ref[:, sl])
        act_ref[:, sl] = (a * jax.nn.sigmoid(a) * b).astype(BF16)
    f = _dot(act_ref[...], wd_ref[...])
    o_ref[...] = x + MACARON_WEIGHT * _rms(f, post_g_ref[...])


def _ffn(x2d, pre_g, wg, wu, wd, post_g, tm):
    n, d = x2d.shape
    row = pl.BlockSpec((tm, d), lambda i: (i, 0))
    return pl.pallas_call(
        _ffn_kernel,
        grid=(n // tm,),
        in_specs=[row, _const_spec((1, d)), _const_spec((d, D_FF)), _const_spec((d, D_FF)),
                  _const_spec((D_FF, d)), _const_spec((1, d))],
        out_specs=row,
        out_shape=jax.ShapeDtypeStruct((n, d), F32),
        scratch_shapes=[pltpu.VMEM((tm, D_FF), BF16)],
        compiler_params=_params(("parallel",)),
        name="ffn",
    )(x2d, pre_g, wg, wu, wd, post_g)


def _proj_kernel(x_ref, g_ref, w_in_ref, w_bg_ref, b_bg_ref,
                 q_ref, k_ref, v_ref, xl_ref, yl_ref, qc_ref, gate_ref):
    d = D_MODEL
    h = _rms(x_ref[...], g_ref[...]).astype(BF16)
    outs = (q_ref, k_ref, v_ref, xl_ref, yl_ref, qc_ref)
    scales = (DA_QK_DIM ** -0.5 * LOG2E, 1.0, 1.0, 1.0, 1.0, CA_HEAD_DIM ** -0.5)
    for j, (o_ref, sc) in enumerate(zip(outs, scales)):
        p = _dot(h, w_in_ref[:, j * d:(j + 1) * d])
        if sc != 1.0:
            p = p * sc
        o_ref[...] = p.astype(o_ref.dtype)
    for j in range(3):
        sl = slice(j * d, (j + 1) * d)
        z = _dot(h, w_bg_ref[:, sl]) + b_bg_ref[:, sl]
        gate_ref[:, sl] = jax.nn.sigmoid(z).astype(gate_ref.dtype)


def _proj(x2d, g, w_in, w_bg, b_bg, tm):
    n, d = x2d.shape
    row = pl.BlockSpec((tm, d), lambda i: (i, 0))
    bf = jax.ShapeDtypeStruct((n, d), BF16)
    f32 = jax.ShapeDtypeStruct((n, d), F32)
    return pl.pallas_call(
        _proj_kernel,
        grid=(n // tm,),
        in_specs=[row, _const_spec((1, d)), _const_spec(w_in.shape), _const_spec(w_bg.shape),
                  _const_spec((1, 3 * d))],
        out_specs=[row, row, row, row, row, row, pl.BlockSpec((tm, 3 * d), lambda i: (i, 0))],
        out_shape=[bf, bf, bf, f32, f32, bf, jax.ShapeDtypeStruct((n, 3 * d), BF16)],
        compiler_params=_params(("parallel",)),
        name="proj",
    )(x2d, g, w_in, w_bg, b_bg)


def _proj_lru_kernel(x_ref, g_ref, w_in_ref, w_bg_ref, b_bg_ref, cw_ref, cb_ref, wax_ref, ba_ref, bx_ref,
                     lam_ref, q_ref, k_ref, v_ref, qc_ref, gate_ref, ol_ref, tail_ref, carry_ref, *, tm):
    d = D_MODEL

    @pl.when(pl.program_id(1) == 0)
    def _():
        tail_ref[...] = jnp.zeros(tail_ref.shape, F32)
        carry_ref[...] = jnp.zeros(carry_ref.shape, F32)

    h = _rms(x_ref[0], g_ref[...]).astype(BF16)
    x_l = _dot(h, w_in_ref[:, 3 * d:4 * d])
    y_l = _dot(h, w_in_ref[:, 4 * d:5 * d])

    def proj_out(j, o_ref, scale):
        p = _dot(h, w_in_ref[:, j * d:(j + 1) * d])
        o_ref[0] = (p if scale == 1.0 else p * scale).astype(o_ref.dtype)

    def gate_out(j):
        sl = slice(j * d, (j + 1) * d)
        z = _dot(h, w_bg_ref[:, sl]) + b_bg_ref[:, sl]
        gate_ref[0, :, sl] = jax.nn.sigmoid(z).astype(gate_ref.dtype)

    other = [functools.partial(proj_out, 0, q_ref, DA_QK_DIM ** -0.5 * LOG2E),
             functools.partial(proj_out, 1, k_ref, 1.0),
             functools.partial(proj_out, 2, v_ref, 1.0),
             functools.partial(proj_out, 5, qc_ref, CA_HEAD_DIM ** -0.5),
             functools.partial(gate_out, 0), functools.partial(gate_out, 1), functools.partial(gate_out, 2)]

    nlam = -lam_ref[...]
    softplus = jnp.maximum(nlam, 0.0) + jnp.log1p(jnp.exp(-jnp.abs(nlam)))
    row8 = lax.broadcasted_iota(jnp.int32, (SUBLANES, LRU_BLOCK), 0)
    for n in range(LRU_BLOCKS):
        sl = slice(n * LRU_BLOCK, (n + 1) * LRU_BLOCK)
        groups = [slice(g * SUBLANES, (g + 1) * SUBLANES) for g in range(tm // SUBLANES)]
        taps = [cw_ref[t:t + 1, sl] for t in range(CONV_WIDTH)]
        bias_c, bias_a, bias_x, sp = cb_ref[:, sl], ba_ref[:, sl], bx_ref[:, sl], softplus[:, sl]

        prev_rolled = [pltpu.roll(tail_ref[:, sl], back, axis=0) for back in range(1, CONV_WIDTH)]
        xcs = []
        for rows in groups:
            cur = x_l[rows, sl]
            xc = bias_c + cur * taps[CONV_WIDTH - 1]
            for back in range(1, CONV_WIDTH):
                rolled = pltpu.roll(cur, back, axis=0)
                xc = xc + jnp.where(row8 < back, prev_rolled[back - 1], rolled) * taps[CONV_WIDTH - 1 - back]
                prev_rolled[back - 1] = rolled
            xcs.append(xc)
        tail_ref[:, sl] = x_l[groups[-1], sl]

        z = _dot(jnp.concatenate(xcs, axis=0).astype(BF16), wax_ref[n])

        carry = carry_ref[:, sl]
        outs = []
        for rows, xc in zip(groups, xcs):
            r = jax.nn.sigmoid(z[rows, :LRU_BLOCK] + bias_a)
            i = jax.nn.sigmoid(z[rows, LRU_BLOCK:] + bias_x)
            ag = jnp.exp(-LRU_C * r * sp)
            gap = 1.0 - ag * ag
            ug = jnp.where(gap > 0.0, gap * lax.rsqrt(gap), 0.0) * (i * xc)
            for dist in (1, 2, 4):
                ok = row8 >= dist
                a_prev = jnp.where(ok, pltpu.roll(ag, dist, axis=0), 1.0)
                u_prev = jnp.where(ok, pltpu.roll(ug, dist, axis=0), 0.0)
                ug = ag * u_prev + ug
                ag = ag * a_prev
            hg = ag * carry + ug
            carry = jnp.broadcast_to(hg[SUBLANES - 1:SUBLANES, :], (SUBLANES, LRU_BLOCK))
            outs.append(hg * jax.nn.gelu(y_l[rows, sl]))
        carry_ref[:, sl] = carry
        ol_ref[0, :, sl] = jnp.concatenate(outs, axis=0).astype(ol_ref.dtype)
        if n < len(other):
            other[n]()


def _proj_lru(x, g, w_in, w_bg, b_bg, conv_w, conv_b, wax, b_a, b_x, lam, tm):
    b, s, d = x.shape
    row = pl.BlockSpec((1, tm, d), lambda bi, si: (bi, si, 0))
    bf = jax.ShapeDtypeStruct((b, s, d), BF16)
    return pl.pallas_call(
        functools.partial(_proj_lru_kernel, tm=tm),
        grid=(b, s // tm),
        in_specs=[row, _const_spec((1, d)), _const_spec(w_in.shape), _const_spec(w_bg.shape),
                  _const_spec((1, 3 * d)), _const_spec((CONV_WIDTH, d)), _const_spec((1, d)),
                  _const_spec(wax.shape), _const_spec((1, d)), _const_spec((1, d)), _const_spec((1, d))],
        out_specs=[row, row, row, row, pl.BlockSpec((1, tm, 3 * d), lambda bi, si: (bi, si, 0)), row],
        out_shape=[bf, bf, bf, bf, jax.ShapeDtypeStruct((b, s, 3 * d), BF16), bf],
        scratch_shapes=[pltpu.VMEM((SUBLANES, d), F32), pltpu.VMEM((SUBLANES, d), F32)],
        compiler_params=_params(("parallel", "arbitrary")),
        name="proj_lru",
    )(x, g, w_in, w_bg, b_bg, conv_w, conv_b, wax, b_a, b_x, lam)


def _diff_attn_kernel(slopes_ref, zero_ref, q_ref, k_ref, v_ref, lq1_ref, lk1_ref, lq2_ref, lk2_ref, hg_ref, o_ref,
                      vaug_ref, qm_ref, m_ref, acc_ref, s0_ref, s1_ref, p0_ref, p1_ref, a0_ref, a1_ref, *, tq):
    seq = q_ref.shape[1]
    dv = DA_V_DIM
    slope = slopes_ref[pl.program_id(1)] * LOG2E
    n_col_tiles = tq // LANES
    s_refs, p_refs, alpha_refs = (s0_ref, s1_ref), (p0_ref, p1_ref), (a0_ref, a1_ref)
    dyn0 = zero_ref[0]

    vaug_ref[:, :dv] = v_ref[0]
    vaug_ref[:, dv:] = jnp.ones((seq, dv), BF16)
    lam = (jnp.exp(jnp.sum(lq1_ref[...] * lk1_ref[...], axis=-1, keepdims=True))
           - jnp.exp(jnp.sum(lq2_ref[...] * lk2_ref[...], axis=-1, keepdims=True)) + LAM_INIT)

    lane = lax.broadcasted_iota(jnp.int32, (1, 2 * DA_QK_DIM), 1)
    col = lax.broadcasted_iota(jnp.int32, (1, tq), 1)
    rr = lax.broadcasted_iota(jnp.int32, (ROW_GROUP, LANES), 0)
    cc = lax.broadcasted_iota(jnp.int32, (ROW_GROUP, LANES), 1)
    pieces = [(qi, j) for qi in range(seq // tq) for j in range(qi + 1)]

    def load_queries(qi):
        q = q_ref[0, qi * tq:(qi + 1) * tq, :]
        zero = jnp.zeros_like(q)
        qm_ref[qi % 2, :tq, :] = jnp.where(lane < DA_QK_DIM, q, zero)
        qm_ref[qi % 2, tq:, :] = jnp.where(lane >= DA_QK_DIM, q, zero)

    def start_scores(n):
        qi, j = pieces[n]
        s_refs[n % 2][...] = _dot_nt(qm_ref[qi % 2], k_ref[0, j * tq:(j + 1) * tq, :])

    def softmax_pv(n):
        qi, j = pieces[n]
        buf = n % 2
        first, diagonal = j == 0, j == qi
        bias = slope * (col + (j - qi) * tq).astype(F32)
        for g in range(2 * tq // ROW_GROUP):
            r0 = g * ROW_GROUP
            rq = r0 % tq
            rows = pl.ds(pl.multiple_of(r0 + dyn0, ROW_GROUP), ROW_GROUP)
            tiles = []
            for c in range(n_col_tiles):
                c0 = c * LANES
                t = s_refs[buf][rows, c0:c0 + LANES] + bias[:, c0:c0 + LANES]
                if diagonal and c0 + LANES - 1 > rq:
                    t = jnp.where(rr + rq >= cc + c0, t, -jnp.inf)
                tiles.append(t)
            mx = jnp.max(functools.reduce(jnp.maximum, tiles), axis=-1, keepdims=True)
            if first:
                m_new = jnp.broadcast_to(mx, (ROW_GROUP, LANES))
            else:
                m_old = m_ref[rows, :]
                m_new = jnp.maximum(m_old, mx)
                alpha_refs[buf][rows, :] = jnp.exp2(m_old - m_new)
            m_ref[rows, :] = m_new
            for c in range(n_col_tiles):
                c0 = c * LANES
                p_refs[buf][rows, c0:c0 + LANES] = jnp.exp2(tiles[c] - m_new).astype(BF16)
        pv = _dot(p_refs[buf][...], vaug_ref[j * tq:(j + 1) * tq, :])
        if first:
            acc_ref[...] = pv
        else:
            alpha = alpha_refs[buf][...]
            acc_ref[:, :dv] = alpha * acc_ref[:, :dv] + pv[:, :dv]
            acc_ref[:, dv:] = alpha * acc_ref[:, dv:] + pv[:, dv:]

    def finalize(qi):
        o = (acc_ref[:tq, :dv] / acc_ref[:tq, dv:]) - lam * (acc_ref[tq:, :dv] / acc_ref[tq:, dv:])
        o_ref[0, qi * tq:(qi + 1) * tq, :] = (_rms(o, hg_ref[...]) * (1.0 - LAM_INIT)).astype(o_ref.dtype)

    load_queries(0)
    start_scores(0)
    for n, (qi, j) in enumerate(pieces):
        if n + 1 < len(pieces):
            if pieces[n + 1][0] != qi:
                load_queries(pieces[n + 1][0])
            start_scores(n + 1)
        softmax_pv(n)
        if j == qi:
            finalize(qi)


def _diff_attn(slopes, q, k, v, lq1, lk1, lq2, lk2, head_g, tq):
    b, s, d = q.shape
    dv = DA_V_DIM
    head = pl.BlockSpec((1, s, dv), lambda bi, hi: (bi, 0, hi))
    vec = lambda w: pl.BlockSpec((1, w), lambda bi, hi: (0, 0))
    return pl.pallas_call(
        functools.partial(_diff_attn_kernel, tq=tq),
        grid=(b, DA_HEADS),
        in_specs=[pl.BlockSpec(memory_space=pltpu.SMEM), pl.BlockSpec(memory_space=pltpu.SMEM), head, head, head,
                  vec(DA_QK_DIM), vec(DA_QK_DIM), vec(DA_QK_DIM), vec(DA_QK_DIM), vec(dv)],
        out_specs=head,
        out_shape=jax.ShapeDtypeStruct((b, s, d), BF16),
        scratch_shapes=[pltpu.VMEM((s, 2 * dv), BF16),
                        pltpu.VMEM((2, 2 * tq, dv), BF16),
                        pltpu.VMEM((2 * tq, LANES), F32),
                        pltpu.VMEM((2 * tq, 2 * dv), F32),
                        pltpu.VMEM((2 * tq, tq), F32), pltpu.VMEM((2 * tq, tq), F32),
                        pltpu.VMEM((2 * tq, tq), BF16), pltpu.VMEM((2 * tq, tq), BF16),
                        pltpu.VMEM((2 * tq, LANES), F32), pltpu.VMEM((2 * tq, LANES), F32)],
        compiler_params=_params(("parallel", "parallel")),
        name="diff_attn",
    )(slopes, jnp.zeros((1,), jnp.int32), q, k, v, lq1, lk1, lq2, lk2, head_g)


def _lru_kernel(x_ref, y_ref, cw_ref, cb_ref, wax_ref, ba_ref, bx_ref, lam_ref, o_ref,
                xbuf_ref, a_ref, u_ref, h_ref, carry_ref, *, tm):
    si = pl.program_id(1)
    w = x_ref.shape[-1]

    @pl.when(si == 0)
    def _():
        xbuf_ref[0:SUBLANES, :] = jnp.zeros((SUBLANES, w), F32)
        carry_ref[...] = jnp.zeros((SUBLANES, w), F32)

    x = x_ref[0]
    xbuf_ref[SUBLANES:, :] = x
    xc = cb_ref[...] + x * cw_ref[CONV_WIDTH - 1:CONV_WIDTH, :]
    for back in range(1, CONV_WIDTH):
        t = CONV_WIDTH - 1 - back
        xc = xc + xbuf_ref[SUBLANES - back:SUBLANES - back + tm, :] * cw_ref[t:t + 1, :]
    xbuf_ref[0:SUBLANES, :] = x[tm - SUBLANES:, :]

    nlam = -lam_ref[...]
    softplus = jnp.maximum(nlam, 0.0) + jnp.log1p(jnp.exp(-jnp.abs(nlam)))
    xcb = xc.astype(BF16)
    for n in range(LRU_BLOCKS):
        sl = slice(n * LRU_BLOCK, (n + 1) * LRU_BLOCK)
        z = _dot(xcb[:, sl], wax_ref[n])
        r = jax.nn.sigmoid(z[:, :LRU_BLOCK] + ba_ref[:, sl])
        i = jax.nn.sigmoid(z[:, LRU_BLOCK:] + bx_ref[:, sl])
        a = jnp.exp(-LRU_C * r * softplus[:, sl])
        a_ref[:, sl] = a
        gap = 1.0 - a * a
        root = jnp.where(gap > 0.0, gap * lax.rsqrt(gap), 0.0)
        u_ref[:, sl] = root * (i * xc[:, sl])

    row = lax.broadcasted_iota(jnp.int32, (SUBLANES, w), 0)

    def group(g, carry):
        rows = pl.ds(pl.multiple_of(g * SUBLANES, SUBLANES), SUBLANES)
        a = a_ref[rows, :]
        u = u_ref[rows, :]
        for dist in (1, 2, 4):
            ok = row >= dist
            a_prev = jnp.where(ok, pltpu.roll(a, dist, axis=0), 1.0)
            u_prev = jnp.where(ok, pltpu.roll(u, dist, axis=0), 0.0)
            u = a * u_prev + u
            a = a * a_prev
        hg = a * carry + u
        h_ref[rows, :] = hg
        return jnp.broadcast_to(hg[SUBLANES - 1:SUBLANES, :], (SUBLANES, w))

    carry_ref[...] = lax.fori_loop(0, tm // SUBLANES, group, carry_ref[...], unroll=4)
    o_ref[0] = (h_ref[...] * jax.nn.gelu(y_ref[0])).astype(o_ref.dtype)


def _lru(x_lru, y_lru, conv_w, conv_b, wax, b_a, b_x, lam, tm):
    b, s, w = x_lru.shape
    row = pl.BlockSpec((1, tm, w), lambda bi, si: (bi, si, 0))
    vec = pl.BlockSpec((1, w), lambda bi, si: (0, 0))
    return pl.pallas_call(
        functools.partial(_lru_kernel, tm=tm),
        grid=(b, s // tm),
        in_specs=[row, row, pl.BlockSpec((CONV_WIDTH, w), lambda bi, si: (0, 0)), vec,
                  pl.BlockSpec(wax.shape, lambda bi, si: (0, 0, 0)), vec, vec, vec],
        out_specs=row,
        out_shape=jax.ShapeDtypeStruct((b, s, w), BF16),
        scratch_shapes=[pltpu.VMEM((tm + SUBLANES, w), F32), pltpu.VMEM((tm, w), F32),
                        pltpu.VMEM((tm, w), F32), pltpu.VMEM((tm, w), F32),
                        pltpu.VMEM((SUBLANES, w), F32)],
        compiler_params=_params(("parallel", "arbitrary")),
        name="lru",
    )(x_lru, y_lru, conv_w, conv_b, wax, b_a, b_x, lam)


def _mem_kv_kernel(mem_ref, g_ref, w_ref, km_ref, vm_ref):
    mn = _rms(mem_ref[0], g_ref[...]).astype(BF16)
    d = D_MODEL
    km_ref[0] = _dot(mn, w_ref[:, :d]).astype(km_ref.dtype)
    vm_ref[0] = _dot(mn, w_ref[:, d:]).astype(vm_ref.dtype)


def _mem_kv(mem, g, w):
    b, m, d = mem.shape
    blk = pl.BlockSpec((1, m, d), lambda bi: (bi, 0, 0))
    out = jax.ShapeDtypeStruct((b, m, d), BF16)
    return pl.pallas_call(
        _mem_kv_kernel,
        grid=(b,),
        in_specs=[blk, _const_spec((1, d)), _const_spec(w.shape)],
        out_specs=[blk, blk],
        out_shape=[out, out],
        compiler_params=_params(("parallel",)),
        name="mem_kv",
    )(mem, g, w)


def _merge_kernel(x_ref, oda_ref, olru_ref, qc_ref, km_ref, vm_ref, gate_ref,
                  wda_ref, wlru_ref, wca_ref, wmix_ref, g_ref, o_ref, oca_ref):
    d = D_MODEL
    qc = qc_ref[0]
    for hh in range(CA_HEADS):
        sl = slice(hh * CA_HEAD_DIM, (hh + 1) * CA_HEAD_DIM)
        s = _dot_nt(qc[:, sl], km_ref[0, :, sl])
        p = jnp.exp(s - jnp.max(s, axis=-1, keepdims=True))
        o = _dot(p.astype(BF16), vm_ref[0, :, sl]) / jnp.sum(p, axis=-1, keepdims=True)
        oca_ref[:, sl] = o.astype(BF16)
    merged = (gate_ref[0, :, 0:d].astype(F32) * _dot(oda_ref[0], wda_ref[...])
              + gate_ref[0, :, d:2 * d].astype(F32) * _dot(olru_ref[0], wlru_ref[...])
              + gate_ref[0, :, 2 * d:].astype(F32) * _dot(oca_ref[...], wca_ref[...]))
    out = _dot(merged.astype(BF16), wmix_ref[...])
    o_ref[0] = x_ref[0] + _rms(out, g_ref[...])


def _merge(x, o_da, o_lru, q_ca, km, vm, gates, w_da, w_lru, w_ca, w_mix, g, tm):
    b, s, d = x.shape
    row = pl.BlockSpec((1, tm, d), lambda bi, si: (bi, si, 0))
    memspec = pl.BlockSpec((1, N_MEM, d), lambda bi, si: (bi, 0, 0))
    wspec = _const_spec((d, d))
    return pl.pallas_call(
        _merge_kernel,
        grid=(b, s // tm),
        in_specs=[row, row, row, row, memspec, memspec,
                  pl.BlockSpec((1, tm, 3 * d), lambda bi, si: (bi, si, 0)),
                  wspec, wspec, wspec, wspec, _const_spec((1, d))],
        out_specs=row,
        out_shape=jax.ShapeDtypeStruct((b, s, d), F32),
        scratch_shapes=[pltpu.VMEM((tm, d), BF16)],
        compiler_params=_params(("parallel", "parallel")),
        name="merge",
    )(x, o_da, o_lru, q_ca, km, vm, gates, w_da, w_lru, w_ca, w_mix, g)


def kernel(x, mem, ffn1_pre_g, ffn1_w_gate, ffn1_w_up, ffn1_w_down, ffn1_post_g, mix_pre_g, w_in, da_lambda_q1, da_lambda_k1, da_lambda_q2, da_lambda_k2, da_head_g, w_da_out, lru_conv_w, lru_conv_b, lru_w_a, lru_b_a, lru_w_x, lru_b_x, lru_lambda, w_lru_out, mem_g, w_mem_kv, w_ca_out, w_branch_gate, b_branch_gate, w_mix_out, mix_post_g, ffn2_pre_g, ffn2_w_gate, ffn2_w_up, ffn2_w_down, ffn2_post_g):
    b, s, d = x.shape
    n = b * s
    bf = lambda w: w.astype(BF16)
    slopes = jnp.exp2(-8.0 * jnp.arange(1, DA_HEADS + 1, dtype=F32) / DA_HEADS)

    x2d = x.reshape(n, d)
    for l in range(ffn1_pre_g.shape[0]):
        x2d = _ffn(x2d, ffn1_pre_g[l][None], bf(ffn1_w_gate[l]), bf(ffn1_w_up[l]), bf(ffn1_w_down[l]),
                   ffn1_post_g[l][None], tm=512)
        r3 = lambda t: t.reshape(b, s, -1)
        wax = bf(jnp.concatenate([lru_w_a[l], lru_w_x[l]], axis=-1))
        q_da, k_da, v_da, q_ca, gates, o_lru = _proj_lru(
            r3(x2d), mix_pre_g[l][None], bf(w_in[l]), bf(w_branch_gate[l]), b_branch_gate[l][None],
            lru_conv_w[l], lru_conv_b[l][None], wax, lru_b_a[l][None], lru_b_x[l][None], lru_lambda[l][None],
            tm=512)
        o_da = _diff_attn(slopes, q_da, k_da, v_da, da_lambda_q1[l][None], da_lambda_k1[l][None],
                          da_lambda_q2[l][None], da_lambda_k2[l][None], da_head_g[l][None], tq=512)
        km, vm = _mem_kv(mem, mem_g[l][None], bf(w_mem_kv[l]))
        x3 = _merge(r3(x2d), o_da, o_lru, q_ca, km, vm, gates, bf(w_da_out[l]), bf(w_lru_out[l]),
                    bf(w_ca_out[l]), bf(w_mix_out[l]), mix_post_g[l][None], tm=512)
        x2d = _ffn(x3.reshape(n, d), ffn2_pre_g[l][None], bf(ffn2_w_gate[l]), bf(ffn2_w_up[l]),
                   bf(ffn2_w_down[l]), ffn2_post_g[l][None], tm=512)
    return x2d.reshape(b, s, d)
```

```python
import functools
import math

import jax
import jax.numpy as jnp
from jax import lax
from jax.experimental import pallas as pl
from jax.experimental.pallas import tpu as pltpu

D_MODEL = 1024
N_MEM = 256
DA_HEADS = 8
DA_QK_DIM = 64
DA_V_DIM = 128
LRU_BLOCKS = 8
LRU_BLOCK = 128
CONV_WIDTH = 4
LRU_C = 8.0
CA_HEADS = 4
CA_HEAD_DIM = 256
D_FF = 2816
MACARON_WEIGHT = 0.5
EPS = 1e-6
LAM_INIT = 0.8 - 0.6 * math.exp(-0.3 * 0)
LOG2E = math.log2(math.e)

SUBLANES = 8
LANES = 128
ROW_GROUP = 16
FF_CHUNK = 256
FFN_ROWS = 1024
FFN_SPLIT = 2
MIX_ROWS = 512
ATTN_ROWS = 512
VMEM_LIMIT = 56 * 1024 * 1024

F32 = jnp.float32
BF16 = jnp.bfloat16


def _rms(xf, g):
    return xf * lax.rsqrt(jnp.mean(xf * xf, axis=-1, keepdims=True) + EPS) * g


def _dot(a, b):
    return jnp.dot(a, b, preferred_element_type=F32)


def _dot_nt(a, b):
    return lax.dot_general(a, b, (((1,), (1,)), ((), ())), preferred_element_type=F32)


def _const_spec(shape):
    nd = len(shape)
    return pl.BlockSpec(shape, lambda *_: (0,) * nd, pipeline_mode=pl.Buffered(1))


def _params(sem):
    return pltpu.CompilerParams(dimension_semantics=sem, vmem_limit_bytes=VMEM_LIMIT)


def _ffn_kernel(x_ref, pre_g_ref, wg_ref, wu_ref, wd_ref, post_g_ref, o_ref, act_ref):
    half = x_ref.shape[0] // FFN_SPLIT
    for part in range(FFN_SPLIT):
        rows = slice(part * half, (part + 1) * half)
        x = x_ref[rows, :]
        rinv = lax.rsqrt(jnp.mean(x * x, axis=-1, keepdims=True) + EPS)
        xg = (x * pre_g_ref[...]).astype(BF16)
        for c in range(D_FF // FF_CHUNK):
            sl = slice(c * FF_CHUNK, (c + 1) * FF_CHUNK)
            a = rinv * _dot(xg, wg_ref[:, sl])
            b = rinv * _dot(xg, wu_ref[:, sl])
            act_ref[rows, sl] = (a * jax.nn.sigmoid(a) * b).astype(BF16)
        f = _dot(act_ref[rows, :], wd_ref[...])
        o_ref[rows, :] = x + MACARON_WEIGHT * _rms(f, post_g_ref[...])


def _ffn(x2d, pre_g, wg, wu, wd, post_g, tm):
    n, d = x2d.shape
    row = pl.BlockSpec((tm, d), lambda i: (i, 0))
    return pl.pallas_call(
        _ffn_kernel,
        grid=(n // tm,),
        in_specs=[row, _const_spec((1, d)), _const_spec((d, D_FF)), _const_spec((d, D_FF)),
                  _const_spec((D_FF, d)), _const_spec((1, d))],
        out_specs=row,
        out_shape=jax.ShapeDtypeStruct((n, d), F32),
        scratch_shapes=[pltpu.VMEM((tm, D_FF), BF16)],
        compiler_params=_params(("parallel",)),
        name="ffn",
    )(x2d, pre_g, wg, wu, wd, post_g)


def _proj_lru_kernel(x_ref, g_ref, w_in_ref, w_bg_ref, b_bg_ref, cw_ref, cb_ref, wax_ref, ba_ref, bx_ref,
                     lam_ref, q_ref, k_ref, v_ref, qc_ref, gate_ref, ol_ref, tail_ref, carry_ref, *, tm):
    d = D_MODEL

    @pl.when(pl.program_id(1) == 0)
    def _():
        tail_ref[...] = jnp.zeros(tail_ref.shape, F32)
        carry_ref[...] = jnp.zeros(carry_ref.shape, F32)

    h = _rms(x_ref[0], g_ref[...]).astype(BF16)
    x_l = _dot(h, w_in_ref[:, 3 * d:4 * d])
    y_l = _dot(h, w_in_ref[:, 4 * d:5 * d])

    def proj_out(j, o_ref, scale):
        p = _dot(h, w_in_ref[:, j * d:(j + 1) * d])
        o_ref[0] = (p if scale == 1.0 else p * scale).astype(o_ref.dtype)

    def gate_out(j):
        sl = slice(j * d, (j + 1) * d)
        z = _dot(h, w_bg_ref[:, sl]) + b_bg_ref[:, sl]
        gate_ref[0, :, sl] = jax.nn.sigmoid(z).astype(gate_ref.dtype)

    other = [functools.partial(proj_out, 0, q_ref, DA_QK_DIM ** -0.5 * LOG2E),
             functools.partial(proj_out, 1, k_ref, 1.0),
             functools.partial(proj_out, 2, v_ref, 1.0),
             functools.partial(proj_out, 5, qc_ref, CA_HEAD_DIM ** -0.5),
             functools.partial(gate_out, 0), functools.partial(gate_out, 1), functools.partial(gate_out, 2)]

    nlam = -lam_ref[...]
    softplus = jnp.maximum(nlam, 0.0) + jnp.log1p(jnp.exp(-jnp.abs(nlam)))
    row8 = lax.broadcasted_iota(jnp.int32, (SUBLANES, LRU_BLOCK), 0)
    for n in range(LRU_BLOCKS):
        sl = slice(n * LRU_BLOCK, (n + 1) * LRU_BLOCK)
        groups = [slice(g * SUBLANES, (g + 1) * SUBLANES) for g in range(tm // SUBLANES)]
        taps = [cw_ref[t:t + 1, sl] for t in range(CONV_WIDTH)]
        bias_c, bias_a, bias_x, sp = cb_ref[:, sl], ba_ref[:, sl], bx_ref[:, sl], softplus[:, sl]

        prev_rolled = [pltpu.roll(tail_ref[:, sl], back, axis=0) for back in range(1, CONV_WIDTH)]
        xcs = []
        for rows in groups:
            cur = x_l[rows, sl]
            xc = bias_c + cur * taps[CONV_WIDTH - 1]
            for back in range(1, CONV_WIDTH):
                rolled = pltpu.roll(cur, back, axis=0)
                xc = xc + jnp.where(row8 < back, prev_rolled[back - 1], rolled) * taps[CONV_WIDTH - 1 - back]
                prev_rolled[back - 1] = rolled
            xcs.append(xc)
        tail_ref[:, sl] = x_l[groups[-1], sl]

        z = _dot(jnp.concatenate(xcs, axis=0).astype(BF16), wax_ref[n])

        carry = carry_ref[:, sl]
        outs = []
        for rows, xc in zip(groups, xcs):
            r = jax.nn.sigmoid(z[rows, :LRU_BLOCK] + bias_a)
            i = jax.nn.sigmoid(z[rows, LRU_BLOCK:] + bias_x)
            ag = jnp.exp(-LRU_C * r * sp)
            gap = 1.0 - ag * ag
            ug = jnp.where(gap > 0.0, gap * lax.rsqrt(gap), 0.0) * (i * xc)
            for dist in (1, 2, 4):
                ok = row8 >= dist
                a_prev = jnp.where(ok, pltpu.roll(ag, dist, axis=0), 1.0)
                u_prev = jnp.where(ok, pltpu.roll(ug, dist, axis=0), 0.0)
                ug = ag * u_prev + ug
                ag = ag * a_prev
            hg = ag * carry + ug
            carry = jnp.broadcast_to(hg[SUBLANES - 1:SUBLANES, :], (SUBLANES, LRU_BLOCK))
            outs.append(hg * jax.nn.gelu(y_l[rows, sl]))
        carry_ref[:, sl] = carry
        ol_ref[0, :, sl] = jnp.concatenate(outs, axis=0).astype(ol_ref.dtype)
        if n < len(other):
            other[n]()


def _proj_lru(x, g, w_in, w_bg, b_bg, conv_w, conv_b, wax, b_a, b_x, lam, tm):
    b, s, d = x.shape
    row = pl.BlockSpec((1, tm, d), lambda bi, si: (bi, si, 0))
    bf = jax.ShapeDtypeStruct((b, s, d), BF16)
    return pl.pallas_call(
        functools.partial(_proj_lru_kernel, tm=tm),
        grid=(b, s // tm),
        in_specs=[row, _const_spec((1, d)), _const_spec(w_in.shape), _const_spec(w_bg.shape),
                  _const_spec((1, 3 * d)), _const_spec((CONV_WIDTH, d)), _const_spec((1, d)),
                  _const_spec(wax.shape), _const_spec((1, d)), _const_spec((1, d)), _const_spec((1, d))],
        out_specs=[row, row, row, row, pl.BlockSpec((1, tm, 3 * d), lambda bi, si: (bi, si, 0)), row],
        out_shape=[bf, bf, bf, bf, jax.ShapeDtypeStruct((b, s, 3 * d), BF16), bf],
        scratch_shapes=[pltpu.VMEM((SUBLANES, d), F32), pltpu.VMEM((SUBLANES, d), F32)],
        compiler_params=_params(("parallel", "arbitrary")),
        name="proj_lru",
    )(x, g, w_in, w_bg, b_bg, conv_w, conv_b, wax, b_a, b_x, lam)


def _diff_attn_kernel(slopes_ref, zero_ref, q_ref, k_ref, v_ref, lq1_ref, lk1_ref, lq2_ref, lk2_ref, hg_ref, o_ref,
                      vaug_ref, qm_ref, m_ref, acc_ref, s0_ref, s1_ref, p0_ref, p1_ref, a0_ref, a1_ref, *, tq):
    seq = q_ref.shape[1]
    dv = DA_V_DIM
    slope = slopes_ref[pl.program_id(1)] * LOG2E
    n_col_tiles = tq // LANES
    s_refs, p_refs, alpha_refs = (s0_ref, s1_ref), (p0_ref, p1_ref), (a0_ref, a1_ref)
    dyn0 = zero_ref[0]

    vaug_ref[:, :dv] = v_ref[0]
    vaug_ref[:, dv:] = jnp.ones((seq, dv), BF16)
    lam = (jnp.exp(jnp.sum(lq1_ref[...] * lk1_ref[...], axis=-1, keepdims=True))
           - jnp.exp(jnp.sum(lq2_ref[...] * lk2_ref[...], axis=-1, keepdims=True)) + LAM_INIT)

    lane = lax.broadcasted_iota(jnp.int32, (1, 2 * DA_QK_DIM), 1)
    col = lax.broadcasted_iota(jnp.int32, (1, tq), 1)
    rr = lax.broadcasted_iota(jnp.int32, (ROW_GROUP, LANES), 0)
    cc = lax.broadcasted_iota(jnp.int32, (ROW_GROUP, LANES), 1)
    pieces = [(qi, j) for qi in range(seq // tq) for j in range(qi + 1)]

    def load_queries(qi):
        q = q_ref[0, qi * tq:(qi + 1) * tq, :]
        zero = jnp.zeros_like(q)
        qm_ref[qi % 2, :tq, :] = jnp.where(lane < DA_QK_DIM, q, zero)
        qm_ref[qi % 2, tq:, :] = jnp.where(lane >= DA_QK_DIM, q, zero)

    def start_scores(n):
        qi, j = pieces[n]
        s_refs[n % 2][...] = _dot_nt(qm_ref[qi % 2], k_ref[0, j * tq:(j + 1) * tq, :])

    def softmax_pv(n):
        qi, j = pieces[n]
        buf = n % 2
        first, diagonal = j == 0, j == qi
        bias = slope * (col + (j - qi) * tq).astype(F32)
        for g in range(2 * tq // ROW_GROUP):
            r0 = g * ROW_GROUP
            rq = r0 % tq
            rows = pl.ds(pl.multiple_of(r0 + dyn0, ROW_GROUP), ROW_GROUP)
            tiles = []
            for c in range(n_col_tiles):
                c0 = c * LANES
                t = s_refs[buf][rows, c0:c0 + LANES] + bias[:, c0:c0 + LANES]
                if diagonal and c0 + LANES - 1 > rq:
                    t = jnp.where(rr + rq >= cc + c0, t, -jnp.inf)
                tiles.append(t)
            mx = jnp.max(functools.reduce(jnp.maximum, tiles), axis=-1, keepdims=True)
            if first:
                m_new = jnp.broadcast_to(mx, (ROW_GROUP, LANES))
            else:
                m_old = m_ref[rows, :]
                m_new = jnp.maximum(m_old, mx)
                alpha_refs[buf][rows, :] = jnp.exp2(m_old - m_new)
            m_ref[rows, :] = m_new
            for c in range(n_col_tiles):
                c0 = c * LANES
                p_refs[buf][rows, c0:c0 + LANES] = jnp.exp2(tiles[c] - m_new).astype(BF16)
        pv = _dot(p_refs[buf][...], vaug_ref[j * tq:(j + 1) * tq, :])
        if first:
            acc_ref[...] = pv
        else:
            alpha = alpha_refs[buf][...]
            acc_ref[:, :dv] = alpha * acc_ref[:, :dv] + pv[:, :dv]
            acc_ref[:, dv:] = alpha * acc_ref[:, dv:] + pv[:, dv:]

    def finalize(qi):
        o = (acc_ref[:tq, :dv] / acc_ref[:tq, dv:]) - lam * (acc_ref[tq:, :dv] / acc_ref[tq:, dv:])
        o_ref[0, qi * tq:(qi + 1) * tq, :] = (_rms(o, hg_ref[...]) * (1.0 - LAM_INIT)).astype(o_ref.dtype)

    load_queries(0)
    start_scores(0)
    for n, (qi, j) in enumerate(pieces):
        if n + 1 < len(pieces):
            if pieces[n + 1][0] != qi:
                load_queries(pieces[n + 1][0])
            start_scores(n + 1)
        softmax_pv(n)
        if j == qi:
            finalize(qi)


def _diff_attn(slopes, q, k, v, lq1, lk1, lq2, lk2, head_g, tq):
    b, s, d = q.shape
    dv = DA_V_DIM
    head = pl.BlockSpec((1, s, dv), lambda bi, hi: (bi, 0, hi))
    vec = lambda w: pl.BlockSpec((1, w), lambda bi, hi: (0, 0))
    return pl.pallas_call(
        functools.partial(_diff_attn_kernel, tq=tq),
        grid=(b, DA_HEADS),
        in_specs=[pl.BlockSpec(memory_space=pltpu.SMEM), pl.BlockSpec(memory_space=pltpu.SMEM), head, head, head,
                  vec(DA_QK_DIM), vec(DA_QK_DIM), vec(DA_QK_DIM), vec(DA_QK_DIM), vec(dv)],
        out_specs=head,
        out_shape=jax.ShapeDtypeStruct((b, s, d), BF16),
        scratch_shapes=[pltpu.VMEM((s, 2 * dv), BF16),
                        pltpu.VMEM((2, 2 * tq, dv), BF16),
                        pltpu.VMEM((2 * tq, LANES), F32),
                        pltpu.VMEM((2 * tq, 2 * dv), F32),
                        pltpu.VMEM((2 * tq, tq), F32), pltpu.VMEM((2 * tq, tq), F32),
                        pltpu.VMEM((2 * tq, tq), BF16), pltpu.VMEM((2 * tq, tq), BF16),
                        pltpu.VMEM((2 * tq, LANES), F32), pltpu.VMEM((2 * tq, LANES), F32)],
        compiler_params=_params(("parallel", "parallel")),
        name="diff_attn",
    )(slopes, jnp.zeros((1,), jnp.int32), q, k, v, lq1, lk1, lq2, lk2, head_g)


def _mem_kv_kernel(mem_ref, g_ref, w_ref, km_ref, vm_ref):
    mn = _rms(mem_ref[0], g_ref[...]).astype(BF16)
    d = D_MODEL
    km_ref[0] = _dot(mn, w_ref[:, :d]).astype(km_ref.dtype)
    vm_ref[0] = _dot(mn, w_ref[:, d:]).astype(vm_ref.dtype)


def _mem_kv(mem, g, w):
    b, m, d = mem.shape
    blk = pl.BlockSpec((1, m, d), lambda bi: (bi, 0, 0))
    out = jax.ShapeDtypeStruct((b, m, d), BF16)
    return pl.pallas_call(
        _mem_kv_kernel,
        grid=(b,),
        in_specs=[blk, _const_spec((1, d)), _const_spec(w.shape)],
        out_specs=[blk, blk],
        out_shape=[out, out],
        compiler_params=_params(("parallel",)),
        name="mem_kv",
    )(mem, g, w)


def _merge_kernel(x_ref, oda_ref, olru_ref, qc_ref, km_ref, vm_ref, gate_ref,
                  wda_ref, wlru_ref, wca_ref, wmix_ref, g_ref, o_ref, oca_ref):
    d = D_MODEL
    qc = qc_ref[0]
    heads = [slice(hh * CA_HEAD_DIM, (hh + 1) * CA_HEAD_DIM) for hh in range(CA_HEADS)]
    scores = [_dot_nt(qc[:, sl], km_ref[0, :, sl]) for sl in heads]
    y_da = _dot(oda_ref[0], wda_ref[...])
    for sl, s in zip(heads, scores):
        p = jnp.exp(s - jnp.max(s, axis=-1, keepdims=True))
        o = _dot(p.astype(BF16), vm_ref[0, :, sl]) / jnp.sum(p, axis=-1, keepdims=True)
        oca_ref[:, sl] = o.astype(BF16)
    merged = (gate_ref[0, :, 0:d].astype(F32) * y_da
              + gate_ref[0, :, d:2 * d].astype(F32) * _dot(olru_ref[0], wlru_ref[...])
              + gate_ref[0, :, 2 * d:].astype(F32) * _dot(oca_ref[...], wca_ref[...]))
    out = _dot(merged.astype(BF16), wmix_ref[...])
    o_ref[0] = x_ref[0] + _rms(out, g_ref[...])


def _merge(x, o_da, o_lru, q_ca, km, vm, gates, w_da, w_lru, w_ca, w_mix, g, tm):
    b, s, d = x.shape
    row = pl.BlockSpec((1, tm, d), lambda bi, si: (bi, si, 0))
    memspec = pl.BlockSpec((1, N_MEM, d), lambda bi, si: (bi, 0, 0))
    wspec = _const_spec((d, d))
    return pl.pallas_call(
        _merge_kernel,
        grid=(b, s // tm),
        in_specs=[row, row, row, row, memspec, memspec,
                  pl.BlockSpec((1, tm, 3 * d), lambda bi, si: (bi, si, 0)),
                  wspec, wspec, wspec, wspec, _const_spec((1, d))],
        out_specs=row,
        out_shape=jax.ShapeDtypeStruct((b, s, d), F32),
        scratch_shapes=[pltpu.VMEM((tm, d), BF16)],
        compiler_params=_params(("parallel", "parallel")),
        name="merge",
    )(x, o_da, o_lru, q_ca, km, vm, gates, w_da, w_lru, w_ca, w_mix, g)


def kernel(x, mem, ffn1_pre_g, ffn1_w_gate, ffn1_w_up, ffn1_w_down, ffn1_post_g, mix_pre_g, w_in, da_lambda_q1, da_lambda_k1, da_lambda_q2, da_lambda_k2, da_head_g, w_da_out, lru_conv_w, lru_conv_b, lru_w_a, lru_b_a, lru_w_x, lru_b_x, lru_lambda, w_lru_out, mem_g, w_mem_kv, w_ca_out, w_branch_gate, b_branch_gate, w_mix_out, mix_post_g, ffn2_pre_g, ffn2_w_gate, ffn2_w_up, ffn2_w_down, ffn2_post_g):
    b, s, d = x.shape
    n = b * s
    bf = lambda w: w.astype(BF16)
    slopes = jnp.exp2(-8.0 * jnp.arange(1, DA_HEADS + 1, dtype=F32) / DA_HEADS)

    x2d = x.reshape(n, d)
    for l in range(ffn1_pre_g.shape[0]):
        x2d = _ffn(x2d, ffn1_pre_g[l][None], bf(ffn1_w_gate[l]), bf(ffn1_w_up[l]), bf(ffn1_w_down[l]),
                   ffn1_post_g[l][None], tm=FFN_ROWS)
        r3 = lambda t: t.reshape(b, s, -1)
        wax = bf(jnp.concatenate([lru_w_a[l], lru_w_x[l]], axis=-1))
        q_da, k_da, v_da, q_ca, gates, o_lru = _proj_lru(
            r3(x2d), mix_pre_g[l][None], bf(w_in[l]), bf(w_branch_gate[l]), b_branch_gate[l][None],
            lru_conv_w[l], lru_conv_b[l][None], wax, lru_b_a[l][None], lru_b_x[l][None], lru_lambda[l][None],
            tm=MIX_ROWS)
        o_da = _diff_attn(slopes, q_da, k_da, v_da, da_lambda_q1[l][None], da_lambda_k1[l][None],
                          da_lambda_q2[l][None], da_lambda_k2[l][None], da_head_g[l][None], tq=ATTN_ROWS)
        km, vm = _mem_kv(mem, mem_g[l][None], bf(w_mem_kv[l]))
        x3 = _merge(r3(x2d), o_da, o_lru, q_ca, km, vm, gates, bf(w_da_out[l]), bf(w_lru_out[l]),
                    bf(w_ca_out[l]), bf(w_mix_out[l]), mix_post_g[l][None], tm=MIX_ROWS)
        x2d = _ffn(x3.reshape(n, d), ffn2_pre_g[l][None], bf(ffn2_w_gate[l]), bf(ffn2_w_up[l]),
                   bf(ffn2_w_down[l]), ffn2_post_g[l][None], tm=FFN_ROWS)
    return x2d.reshape(b, s, d)
```

```python
import functools
import math

import jax
import jax.numpy as jnp
from jax import lax
from jax.experimental import pallas as pl
from jax.experimental.pallas import tpu as pltpu

D_MODEL = 1024
N_MEM = 256
DA_HEADS = 8
DA_QK_DIM = 64
DA_V_DIM = 128
LRU_BLOCKS = 8
LRU_BLOCK = 128
CONV_WIDTH = 4
LRU_C = 8.0
CA_HEADS = 4
CA_HEAD_DIM = 256
D_FF = 2816
MACARON_WEIGHT = 0.5
EPS = 1e-6
LAM_INIT = 0.8 - 0.6 * math.exp(-0.3 * 0)
LOG2E = math.log2(math.e)

SUBLANES = 8
LANES = 128
ROW_GROUP = 16
FF_CHUNK = 256
FFN_ROWS = 1024
FFN_SPLIT = 2
MIX_ROWS = 512
ATTN_ROWS = 512
VMEM_LIMIT = 56 * 1024 * 1024

F32 = jnp.float32
BF16 = jnp.bfloat16


def _rms(xf, g):
    return xf * lax.rsqrt(jnp.mean(xf * xf, axis=-1, keepdims=True) + EPS) * g


def _dot(a, b):
    return jnp.dot(a, b, preferred_element_type=F32)


def _dot_nt(a, b):
    return lax.dot_general(a, b, (((1,), (1,)), ((), ())), preferred_element_type=F32)


def _const_spec(shape):
    nd = len(shape)
    return pl.BlockSpec(shape, lambda *_: (0,) * nd, pipeline_mode=pl.Buffered(1))


def _params(sem):
    return pltpu.CompilerParams(dimension_semantics=sem, vmem_limit_bytes=VMEM_LIMIT)


def _rider_specs(weights, n_steps, step_of):
    in_specs, out_specs, out_shapes = [], [], []
    for w in weights:
        rows, cols = w.shape
        spec = pl.BlockSpec((rows // n_steps, cols), lambda *g: (step_of(*g), 0))
        in_specs.append(spec)
        out_specs.append(spec)
        out_shapes.append(jax.ShapeDtypeStruct(w.shape, BF16))
    return in_specs, out_specs, out_shapes


def _cast_riders(in_refs, out_refs):
    for w_ref, o_ref in zip(in_refs, out_refs):
        o_ref[...] = w_ref[...].astype(BF16)


def _ffn_kernel(x_ref, pre_g_ref, wg_ref, wu_ref, wd_ref, post_g_ref, *rest, n_riders):
    o_ref, act_ref = rest[n_riders], rest[-1]
    _cast_riders(rest[:n_riders], rest[n_riders + 1:-1])
    half = x_ref.shape[0] // FFN_SPLIT
    for part in range(FFN_SPLIT):
        rows = slice(part * half, (part + 1) * half)
        x = x_ref[rows, :]
        rinv = lax.rsqrt(jnp.mean(x * x, axis=-1, keepdims=True) + EPS)
        xg = (x * pre_g_ref[...]).astype(BF16)
        for c in range(D_FF // FF_CHUNK):
            sl = slice(c * FF_CHUNK, (c + 1) * FF_CHUNK)
            a = rinv * _dot(xg, wg_ref[:, sl])
            b = rinv * _dot(xg, wu_ref[:, sl])
            act_ref[rows, sl] = (a * jax.nn.sigmoid(a) * b).astype(BF16)
        f = _dot(act_ref[rows, :], wd_ref[...])
        o_ref[rows, :] = x + MACARON_WEIGHT * _rms(f, post_g_ref[...])


def _ffn(x2d, pre_g, wg, wu, wd, post_g, tm, riders=()):
    n, d = x2d.shape
    row = pl.BlockSpec((tm, d), lambda i: (i, 0))
    r_in, r_out, r_shapes = _rider_specs(riders, n // tm, lambda i: i)
    return pl.pallas_call(
        functools.partial(_ffn_kernel, n_riders=len(riders)),
        grid=(n // tm,),
        in_specs=[row, _const_spec((1, d)), _const_spec((d, D_FF)), _const_spec((d, D_FF)),
                  _const_spec((D_FF, d)), _const_spec((1, d))] + r_in,
        out_specs=[row] + r_out,
        out_shape=[jax.ShapeDtypeStruct((n, d), F32)] + r_shapes,
        scratch_shapes=[pltpu.VMEM((tm, D_FF), BF16)],
        compiler_params=_params(("parallel",)),
        name="ffn",
    )(x2d, pre_g, wg, wu, wd, post_g, *riders)


def _proj_lru_kernel(x_ref, g_ref, w_in_ref, w_bg_ref, b_bg_ref, cw_ref, cb_ref, wax_ref, ba_ref, bx_ref,
                     lam_ref, *rest, tm, n_riders):
    d = D_MODEL
    q_ref, k_ref, v_ref, qc_ref, gate_ref, ol_ref = rest[n_riders:n_riders + 6]
    tail_ref, carry_ref = rest[-2:]
    _cast_riders(rest[:n_riders], rest[n_riders + 6:-2])

    @pl.when(pl.program_id(1) == 0)
    def _():
        tail_ref[...] = jnp.zeros(tail_ref.shape, F32)
        carry_ref[...] = jnp.zeros(carry_ref.shape, F32)

    h = _rms(x_ref[0], g_ref[...]).astype(BF16)
    x_l = _dot(h, w_in_ref[:, 3 * d:4 * d])
    y_l = _dot(h, w_in_ref[:, 4 * d:5 * d])

    def proj_out(j, o_ref, scale):
        p = _dot(h, w_in_ref[:, j * d:(j + 1) * d])
        o_ref[0] = (p if scale == 1.0 else p * scale).astype(o_ref.dtype)

    def gate_out(j):
        sl = slice(j * d, (j + 1) * d)
        z = _dot(h, w_bg_ref[:, sl]) + b_bg_ref[:, sl]
        gate_ref[0, :, sl] = jax.nn.sigmoid(z).astype(gate_ref.dtype)

    other = [functools.partial(proj_out, 0, q_ref, DA_QK_DIM ** -0.5 * LOG2E),
             functools.partial(proj_out, 1, k_ref, 1.0),
             functools.partial(proj_out, 2, v_ref, 1.0),
             functools.partial(proj_out, 5, qc_ref, CA_HEAD_DIM ** -0.5),
             functools.partial(gate_out, 0), functools.partial(gate_out, 1), functools.partial(gate_out, 2)]

    nlam = -lam_ref[...]
    softplus = jnp.maximum(nlam, 0.0) + jnp.log1p(jnp.exp(-jnp.abs(nlam)))
    row8 = lax.broadcasted_iota(jnp.int32, (SUBLANES, LRU_BLOCK), 0)
    for n in range(LRU_BLOCKS):
        sl = slice(n * LRU_BLOCK, (n + 1) * LRU_BLOCK)
        groups = [slice(g * SUBLANES, (g + 1) * SUBLANES) for g in range(tm // SUBLANES)]
        taps = [cw_ref[t:t + 1, sl] for t in range(CONV_WIDTH)]
        bias_c, bias_a, bias_x, sp = cb_ref[:, sl], ba_ref[:, sl], bx_ref[:, sl], softplus[:, sl]

        prev_rolled = [pltpu.roll(tail_ref[:, sl], back, axis=0) for back in range(1, CONV_WIDTH)]
        xcs = []
        for rows in groups:
            cur = x_l[rows, sl]
            xc = bias_c + cur * taps[CONV_WIDTH - 1]
            for back in range(1, CONV_WIDTH):
                rolled = pltpu.roll(cur, back, axis=0)
                xc = xc + jnp.where(row8 < back, prev_rolled[back - 1], rolled) * taps[CONV_WIDTH - 1 - back]
                prev_rolled[back - 1] = rolled
            xcs.append(xc)
        tail_ref[:, sl] = x_l[groups[-1], sl]

        z = _dot(jnp.concatenate(xcs, axis=0).astype(BF16), wax_ref[n])

        carry = carry_ref[:, sl]
        outs = []
        for rows, xc in zip(groups, xcs):
            r = jax.nn.sigmoid(z[rows, :LRU_BLOCK] + bias_a)
            i = jax.nn.sigmoid(z[rows, LRU_BLOCK:] + bias_x)
            ag = jnp.exp(-LRU_C * r * sp)
            gap = 1.0 - ag * ag
            ug = jnp.where(gap > 0.0, gap * lax.rsqrt(gap), 0.0) * (i * xc)
            for dist in (1, 2, 4):
                ok = row8 >= dist
                a_prev = jnp.where(ok, pltpu.roll(ag, dist, axis=0), 1.0)
                u_prev = jnp.where(ok, pltpu.roll(ug, dist, axis=0), 0.0)
                ug = ag * u_prev + ug
                ag = ag * a_prev
            hg = ag * carry + ug
            carry = jnp.broadcast_to(hg[SUBLANES - 1:SUBLANES, :], (SUBLANES, LRU_BLOCK))
            outs.append(hg * jax.nn.gelu(y_l[rows, sl]))
        carry_ref[:, sl] = carry
        ol_ref[0, :, sl] = jnp.concatenate(outs, axis=0).astype(ol_ref.dtype)
        if n < len(other):
            other[n]()


def _proj_lru(x, g, w_in, w_bg, b_bg, conv_w, conv_b, wax, b_a, b_x, lam, tm, riders=()):
    b, s, d = x.shape
    row = pl.BlockSpec((1, tm, d), lambda bi, si: (bi, si, 0))
    bf = jax.ShapeDtypeStruct((b, s, d), BF16)
    tiles = s // tm
    r_in, r_out, r_shapes = _rider_specs(riders, b * tiles, lambda bi, si: bi * tiles + si)
    return pl.pallas_call(
        functools.partial(_proj_lru_kernel, tm=tm, n_riders=len(riders)),
        grid=(b, tiles),
        in_specs=[row, _const_spec((1, d)), _const_spec(w_in.shape), _const_spec(w_bg.shape),
                  _const_spec((1, 3 * d)), _const_spec((CONV_WIDTH, d)), _const_spec((1, d)),
                  _const_spec(wax.shape), _const_spec((1, d)), _const_spec((1, d)), _const_spec((1, d))] + r_in,
        out_specs=[row, row, row, row, pl.BlockSpec((1, tm, 3 * d), lambda bi, si: (bi, si, 0)), row] + r_out,
        out_shape=[bf, bf, bf, bf, jax.ShapeDtypeStruct((b, s, 3 * d), BF16), bf] + r_shapes,
        scratch_shapes=[pltpu.VMEM((SUBLANES, d), F32), pltpu.VMEM((SUBLANES, d), F32)],
        compiler_params=_params(("parallel", "arbitrary")),
        name="proj_lru",
    )(x, g, w_in, w_bg, b_bg, conv_w, conv_b, wax, b_a, b_x, lam, *riders)


def _diff_attn_kernel(slopes_ref, zero_ref, q_ref, k_ref, v_ref, lq1_ref, lk1_ref, lq2_ref, lk2_ref, hg_ref, o_ref,
                      vaug_ref, qm_ref, m_ref, acc_ref, s0_ref, s1_ref, p0_ref, p1_ref, a0_ref, a1_ref, *, tq):
    seq = q_ref.shape[1]
    dv = DA_V_DIM
    slope = slopes_ref[pl.program_id(1)] * LOG2E
    n_col_tiles = tq // LANES
    s_refs, p_refs, alpha_refs = (s0_ref, s1_ref), (p0_ref, p1_ref), (a0_ref, a1_ref)
    dyn0 = zero_ref[0]

    vaug_ref[:, :dv] = v_ref[0]
    vaug_ref[:, dv:] = jnp.ones((seq, dv), BF16)
    lam = (jnp.exp(jnp.sum(lq1_ref[...] * lk1_ref[...], axis=-1, keepdims=True))
           - jnp.exp(jnp.sum(lq2_ref[...] * lk2_ref[...], axis=-1, keepdims=True)) + LAM_INIT)

    lane = lax.broadcasted_iota(jnp.int32, (1, 2 * DA_QK_DIM), 1)
    col = lax.broadcasted_iota(jnp.int32, (1, tq), 1)
    rr = lax.broadcasted_iota(jnp.int32, (ROW_GROUP, LANES), 0)
    cc = lax.broadcasted_iota(jnp.int32, (ROW_GROUP, LANES), 1)
    pieces = [(qi, j) for qi in range(seq // tq) for j in range(qi + 1)]

    def load_queries(qi):
        q = q_ref[0, qi * tq:(qi + 1) * tq, :]
        zero = jnp.zeros_like(q)
        qm_ref[qi % 2, :tq, :] = jnp.where(lane < DA_QK_DIM, q, zero)
        qm_ref[qi % 2, tq:, :] = jnp.where(lane >= DA_QK_DIM, q, zero)

    def start_scores(n):
        qi, j = pieces[n]
        s_refs[n % 2][...] = _dot_nt(qm_ref[qi % 2], k_ref[0, j * tq:(j + 1) * tq, :])

    def softmax_pv(n):
        qi, j = pieces[n]
        buf = n % 2
        first, diagonal = j == 0, j == qi
        bias = slope * (col + (j - qi) * tq).astype(F32)
        for g in range(2 * tq // ROW_GROUP):
            r0 = g * ROW_GROUP
            rq = r0 % tq
            rows = pl.ds(pl.multiple_of(r0 + dyn0, ROW_GROUP), ROW_GROUP)
            tiles = []
            for c in range(n_col_tiles):
                c0 = c * LANES
                t = s_refs[buf][rows, c0:c0 + LANES] + bias[:, c0:c0 + LANES]
                if diagonal and c0 + LANES - 1 > rq:
                    t = jnp.where(rr + rq >= cc + c0, t, -jnp.inf)
                tiles.append(t)
            mx = jnp.max(functools.reduce(jnp.maximum, tiles), axis=-1, keepdims=True)
            if first:
                m_new = jnp.broadcast_to(mx, (ROW_GROUP, LANES))
            else:
                m_old = m_ref[rows, :]
                m_new = jnp.maximum(m_old, mx)
                alpha_refs[buf][rows, :] = jnp.exp2(m_old - m_new)
            m_ref[rows, :] = m_new
            for c in range(n_col_tiles):
                c0 = c * LANES
                p_refs[buf][rows, c0:c0 + LANES] = jnp.exp2(tiles[c] - m_new).astype(BF16)
        pv = _dot(p_refs[buf][...], vaug_ref[j * tq:(j + 1) * tq, :])
        if first:
            acc_ref[...] = pv
        else:
            alpha = alpha_refs[buf][...]
            acc_ref[:, :dv] = alpha * acc_ref[:, :dv] + pv[:, :dv]
            acc_ref[:, dv:] = alpha * acc_ref[:, dv:] + pv[:, dv:]

    def finalize(qi):
        o = (acc_ref[:tq, :dv] / acc_ref[:tq, dv:]) - lam * (acc_ref[tq:, :dv] / acc_ref[tq:, dv:])
        o_ref[0, qi * tq:(qi + 1) * tq, :] = (_rms(o, hg_ref[...]) * (1.0 - LAM_INIT)).astype(o_ref.dtype)

    load_queries(0)
    start_scores(0)
    for n, (qi, j) in enumerate(pieces):
        if n + 1 < len(pieces):
            if pieces[n + 1][0] != qi:
                load_queries(pieces[n + 1][0])
            start_scores(n + 1)
        softmax_pv(n)
        if j == qi:
            finalize(qi)


def _diff_attn(slopes, q, k, v, lq1, lk1, lq2, lk2, head_g, tq):
    b, s, d = q.shape
    dv = DA_V_DIM
    head = pl.BlockSpec((1, s, dv), lambda bi, hi: (bi, 0, hi))
    vec = lambda w: pl.BlockSpec((1, w), lambda bi, hi: (0, 0))
    return pl.pallas_call(
        functools.partial(_diff_attn_kernel, tq=tq),
        grid=(b, DA_HEADS),
        in_specs=[pl.BlockSpec(memory_space=pltpu.SMEM), pl.BlockSpec(memory_space=pltpu.SMEM), head, head, head,
                  vec(DA_QK_DIM), vec(DA_QK_DIM), vec(DA_QK_DIM), vec(DA_QK_DIM), vec(dv)],
        out_specs=head,
        out_shape=jax.ShapeDtypeStruct((b, s, d), BF16),
        scratch_shapes=[pltpu.VMEM((s, 2 * dv), BF16),
                        pltpu.VMEM((2, 2 * tq, dv), BF16),
                        pltpu.VMEM((2 * tq, LANES), F32),
                        pltpu.VMEM((2 * tq, 2 * dv), F32),
                        pltpu.VMEM((2 * tq, tq), F32), pltpu.VMEM((2 * tq, tq), F32),
                        pltpu.VMEM((2 * tq, tq), BF16), pltpu.VMEM((2 * tq, tq), BF16),
                        pltpu.VMEM((2 * tq, LANES), F32), pltpu.VMEM((2 * tq, LANES), F32)],
        compiler_params=_params(("parallel", "parallel")),
        name="diff_attn",
    )(slopes, jnp.zeros((1,), jnp.int32), q, k, v, lq1, lk1, lq2, lk2, head_g)


def _mem_kv_kernel(mem_ref, g_ref, w_ref, km_ref, vm_ref):
    mn = _rms(mem_ref[0], g_ref[...]).astype(BF16)
    d = D_MODEL
    km_ref[0] = _dot(mn, w_ref[:, :d]).astype(km_ref.dtype)
    vm_ref[0] = _dot(mn, w_ref[:, d:]).astype(vm_ref.dtype)


def _mem_kv(mem, g, w):
    b, m, d = mem.shape
    blk = pl.BlockSpec((1, m, d), lambda bi: (bi, 0, 0))
    out = jax.ShapeDtypeStruct((b, m, d), BF16)
    return pl.pallas_call(
        _mem_kv_kernel,
        grid=(b,),
        in_specs=[blk, _const_spec((1, d)), _const_spec(w.shape)],
        out_specs=[blk, blk],
        out_shape=[out, out],
        compiler_params=_params(("parallel",)),
        name="mem_kv",
    )(mem, g, w)


def _merge_kernel(x_ref, oda_ref, olru_ref, qc_ref, km_ref, vm_ref, gate_ref,
                  wda_ref, wlru_ref, wca_ref, wmix_ref, g_ref, *rest, n_riders):
    d = D_MODEL
    o_ref, oca_ref = rest[n_riders], rest[-1]
    _cast_riders(rest[:n_riders], rest[n_riders + 1:-1])
    qc = qc_ref[0]
    heads = [slice(hh * CA_HEAD_DIM, (hh + 1) * CA_HEAD_DIM) for hh in range(CA_HEADS)]
    scores = [_dot_nt(qc[:, sl], km_ref[0, :, sl]) for sl in heads]
    y_da = _dot(oda_ref[0], wda_ref[...])
    for sl, s in zip(heads, scores):
        p = jnp.exp(s - jnp.max(s, axis=-1, keepdims=True))
        o = _dot(p.astype(BF16), vm_ref[0, :, sl]) / jnp.sum(p, axis=-1, keepdims=True)
        oca_ref[:, sl] = o.astype(BF16)
    merged = (gate_ref[0, :, 0:d].astype(F32) * y_da
              + gate_ref[0, :, d:2 * d].astype(F32) * _dot(olru_ref[0], wlru_ref[...])
              + gate_ref[0, :, 2 * d:].astype(F32) * _dot(oca_ref[...], wca_ref[...]))
    out = _dot(merged.astype(BF16), wmix_ref[...])
    o_ref[0] = x_ref[0] + _rms(out, g_ref[...])


def _merge(x, o_da, o_lru, q_ca, km, vm, gates, w_da, w_lru, w_ca, w_mix, g, tm, riders=()):
    b, s, d = x.shape
    row = pl.BlockSpec((1, tm, d), lambda bi, si: (bi, si, 0))
    memspec = pl.BlockSpec((1, N_MEM, d), lambda bi, si: (bi, 0, 0))
    wspec = _const_spec((d, d))
    tiles = s // tm
    r_in, r_out, r_shapes = _rider_specs(riders, b * tiles, lambda bi, si: bi * tiles + si)
    return pl.pallas_call(
        functools.partial(_merge_kernel, n_riders=len(riders)),
        grid=(b, tiles),
        in_specs=[row, row, row, row, memspec, memspec,
                  pl.BlockSpec((1, tm, 3 * d), lambda bi, si: (bi, si, 0)),
                  wspec, wspec, wspec, wspec, _const_spec((1, d))] + r_in,
        out_specs=[row] + r_out,
        out_shape=[jax.ShapeDtypeStruct((b, s, d), F32)] + r_shapes,
        scratch_shapes=[pltpu.VMEM((tm, d), BF16)],
        compiler_params=_params(("parallel", "parallel")),
        name="merge",
    )(x, o_da, o_lru, q_ca, km, vm, gates, w_da, w_lru, w_ca, w_mix, g, *riders)


def kernel(x, mem, ffn1_pre_g, ffn1_w_gate, ffn1_w_up, ffn1_w_down, ffn1_post_g, mix_pre_g, w_in, da_lambda_q1, da_lambda_k1, da_lambda_q2, da_lambda_k2, da_head_g, w_da_out, lru_conv_w, lru_conv_b, lru_w_a, lru_b_a, lru_w_x, lru_b_x, lru_lambda, w_lru_out, mem_g, w_mem_kv, w_ca_out, w_branch_gate, b_branch_gate, w_mix_out, mix_post_g, ffn2_pre_g, ffn2_w_gate, ffn2_w_up, ffn2_w_down, ffn2_post_g):
    b, s, d = x.shape
    n = b * s
    bf = lambda w: w.astype(BF16)
    slopes = jnp.exp2(-8.0 * jnp.arange(1, DA_HEADS + 1, dtype=F32) / DA_HEADS)

    x2d = x.reshape(n, d)
    for l in range(ffn1_pre_g.shape[0]):
        x2d, w_in_b, w_bg_b = _ffn(
            x2d, ffn1_pre_g[l][None], bf(ffn1_w_gate[l]), bf(ffn1_w_up[l]), bf(ffn1_w_down[l]),
            ffn1_post_g[l][None], tm=FFN_ROWS, riders=(w_in[l], w_branch_gate[l]))
        r3 = lambda t: t.reshape(b, s, -1)
        wax = bf(jnp.concatenate([lru_w_a[l], lru_w_x[l]], axis=-1))
        q_da, k_da, v_da, q_ca, gates, o_lru, w_da_b, w_lru_b, w_ca_b, w_mix_b, w_kv_b = _proj_lru(
            r3(x2d), mix_pre_g[l][None], w_in_b, w_bg_b, b_branch_gate[l][None],
            lru_conv_w[l], lru_conv_b[l][None], wax, lru_b_a[l][None], lru_b_x[l][None], lru_lambda[l][None],
            tm=MIX_ROWS, riders=(w_da_out[l], w_lru_out[l], w_ca_out[l], w_mix_out[l], w_mem_kv[l]))
        o_da = _diff_attn(slopes, q_da, k_da, v_da, da_lambda_q1[l][None], da_lambda_k1[l][None],
                          da_lambda_q2[l][None], da_lambda_k2[l][None], da_head_g[l][None], tq=ATTN_ROWS)
        km, vm = _mem_kv(mem, mem_g[l][None], w_kv_b)
        x3, wg2_b, wu2_b, wd2_b = _merge(
            r3(x2d), o_da, o_lru, q_ca, km, vm, gates, w_da_b, w_lru_b, w_ca_b, w_mix_b, mix_post_g[l][None],
            tm=MIX_ROWS, riders=(ffn2_w_gate[l], ffn2_w_up[l], ffn2_w_down[l].reshape(d, D_FF)))
        x2d, = _ffn(x3.reshape(n, d), ffn2_pre_g[l][None], wg2_b, wu2_b, wd2_b.reshape(D_FF, d),
                    ffn2_post_g[l][None], tm=FFN_ROWS)
    return x2d.reshape(b, s, d)
```

```python
import functools
import math

import jax
import jax.numpy as jnp
from jax import lax
from jax.experimental import pallas as pl
from jax.experimental.pallas import tpu as pltpu

D_MODEL = 1024
N_MEM = 256
DA_HEADS = 8
DA_QK_DIM = 64
DA_V_DIM = 128
LRU_BLOCKS = 8
LRU_BLOCK = 128
CONV_WIDTH = 4
LRU_C = 8.0
CA_HEADS = 4
CA_HEAD_DIM = 256
D_FF = 2816
MACARON_WEIGHT = 0.5
EPS = 1e-6
LAM_INIT = 0.8 - 0.6 * math.exp(-0.3 * 0)
LOG2E = math.log2(math.e)

SUBLANES = 8
LANES = 128
ROW_GROUP = 16
FF_CHUNK = 256
FFN_ROWS = 1024
FFN_SPLIT = 2
MIX_ROWS = 512
ATTN_ROWS = 512
VMEM_LIMIT = 56 * 1024 * 1024

F32 = jnp.float32
BF16 = jnp.bfloat16


def _rms(xf, g):
    return xf * lax.rsqrt(jnp.mean(xf * xf, axis=-1, keepdims=True) + EPS) * g


def _dot(a, b):
    return jnp.dot(a, b, preferred_element_type=F32)


def _dot_nt(a, b):
    return lax.dot_general(a, b, (((1,), (1,)), ((), ())), preferred_element_type=F32)


def _const_spec(shape):
    nd = len(shape)
    return pl.BlockSpec(shape, lambda *_: (0,) * nd, pipeline_mode=pl.Buffered(1))


def _params(sem):
    return pltpu.CompilerParams(dimension_semantics=sem, vmem_limit_bytes=VMEM_LIMIT)


def _rider_specs(weights, n_steps, step_of):
    in_specs, out_specs, out_shapes = [], [], []
    for w in weights:
        rows, cols = w.shape
        spec = pl.BlockSpec((rows // n_steps, cols), lambda *g: (step_of(*g), 0))
        in_specs.append(spec)
        out_specs.append(spec)
        out_shapes.append(jax.ShapeDtypeStruct(w.shape, BF16))
    return in_specs, out_specs, out_shapes


def _cast_riders(in_refs, out_refs):
    for w_ref, o_ref in zip(in_refs, out_refs):
        o_ref[...] = w_ref[...].astype(BF16)


def _ffn_kernel(x_ref, pre_g_ref, wg_ref, wu_ref, wd_ref, post_g_ref, *rest, n_riders):
    o_ref, act_ref = rest[n_riders], rest[-1]
    _cast_riders(rest[:n_riders], rest[n_riders + 1:-1])
    half = x_ref.shape[0] // FFN_SPLIT
    for part in range(FFN_SPLIT):
        rows = slice(part * half, (part + 1) * half)
        x = x_ref[rows, :]
        rinv = lax.rsqrt(jnp.mean(x * x, axis=-1, keepdims=True) + EPS)
        xg = (x * pre_g_ref[...]).astype(BF16)
        for c in range(D_FF // FF_CHUNK):
            sl = slice(c * FF_CHUNK, (c + 1) * FF_CHUNK)
            a = rinv * _dot(xg, wg_ref[:, sl])
            b = rinv * _dot(xg, wu_ref[:, sl])
            act_ref[rows, sl] = (a * jax.nn.sigmoid(a) * b).astype(BF16)
        f = _dot(act_ref[rows, :], wd_ref[...])
        o_ref[rows, :] = x + MACARON_WEIGHT * _rms(f, post_g_ref[...])


def _ffn(x2d, pre_g, wg, wu, wd, post_g, tm, riders=()):
    n, d = x2d.shape
    row = pl.BlockSpec((tm, d), lambda i: (i, 0))
    r_in, r_out, r_shapes = _rider_specs(riders, n // tm, lambda i: i)
    return pl.pallas_call(
        functools.partial(_ffn_kernel, n_riders=len(riders)),
        grid=(n // tm,),
        in_specs=[row, _const_spec((1, d)), _const_spec((d, D_FF)), _const_spec((d, D_FF)),
                  _const_spec((D_FF, d)), _const_spec((1, d))] + r_in,
        out_specs=[row] + r_out,
        out_shape=[jax.ShapeDtypeStruct((n, d), F32)] + r_shapes,
        scratch_shapes=[pltpu.VMEM((tm, D_FF), BF16)],
        compiler_params=_params(("parallel",)),
        name="ffn",
    )(x2d, pre_g, wg, wu, wd, post_g, *riders)


def _proj_lru_kernel(x_ref, g_ref, w_in_ref, w_bg_ref, b_bg_ref, cw_ref, cb_ref, wax_ref, ba_ref, bx_ref,
                     lam_ref, *rest, tm, n_riders):
    d = D_MODEL
    q_ref, k_ref, v_ref, qc_ref, gate_ref, ol_ref = rest[n_riders:n_riders + 6]
    tail_ref, carry_ref = rest[-2:]
    _cast_riders(rest[:n_riders], rest[n_riders + 6:-2])

    @pl.when(pl.program_id(1) == 0)
    def _():
        tail_ref[...] = jnp.zeros(tail_ref.shape, F32)
        carry_ref[...] = jnp.zeros(carry_ref.shape, F32)

    h = _rms(x_ref[0], g_ref[...]).astype(BF16)
    x_l = _dot(h, w_in_ref[:, 3 * d:4 * d])
    y_l = _dot(h, w_in_ref[:, 4 * d:5 * d])

    def proj_out(j, o_ref, scale):
        p = _dot(h, w_in_ref[:, j * d:(j + 1) * d])
        o_ref[0] = (p if scale == 1.0 else p * scale).astype(o_ref.dtype)

    def gate_out(j):
        sl = slice(j * d, (j + 1) * d)
        z = _dot(h, w_bg_ref[:, sl]) + b_bg_ref[:, sl]
        gate_ref[0, :, sl] = jax.nn.sigmoid(z).astype(gate_ref.dtype)

    other = [functools.partial(proj_out, 0, q_ref, DA_QK_DIM ** -0.5 * LOG2E),
             functools.partial(proj_out, 1, k_ref, 1.0),
             functools.partial(proj_out, 2, v_ref, 1.0),
             functools.partial(proj_out, 5, qc_ref, CA_HEAD_DIM ** -0.5),
             functools.partial(gate_out, 0), functools.partial(gate_out, 1), functools.partial(gate_out, 2)]

    nlam = -lam_ref[...]
    softplus = jnp.maximum(nlam, 0.0) + jnp.log1p(jnp.exp(-jnp.abs(nlam)))
    row8 = lax.broadcasted_iota(jnp.int32, (SUBLANES, LRU_BLOCK), 0)
    for n in range(LRU_BLOCKS):
        sl = slice(n * LRU_BLOCK, (n + 1) * LRU_BLOCK)
        groups = [slice(g * SUBLANES, (g + 1) * SUBLANES) for g in range(tm // SUBLANES)]
        taps = [cw_ref[t:t + 1, sl] for t in range(CONV_WIDTH)]
        bias_c, bias_a, bias_x, sp = cb_ref[:, sl], ba_ref[:, sl], bx_ref[:, sl], softplus[:, sl]

        prev_rolled = [pltpu.roll(tail_ref[:, sl], back, axis=0) for back in range(1, CONV_WIDTH)]
        xcs = []
        for rows in groups:
            cur = x_l[rows, sl]
            xc = bias_c + cur * taps[CONV_WIDTH - 1]
            for back in range(1, CONV_WIDTH):
                rolled = pltpu.roll(cur, back, axis=0)
                xc = xc + jnp.where(row8 < back, prev_rolled[back - 1], rolled) * taps[CONV_WIDTH - 1 - back]
                prev_rolled[back - 1] = rolled
            xcs.append(xc)
        tail_ref[:, sl] = x_l[groups[-1], sl]

        z = _dot(jnp.concatenate(xcs, axis=0).astype(BF16), wax_ref[n])

        carry = carry_ref[:, sl]
        outs = []
        for rows, xc in zip(groups, xcs):
            r = jax.nn.sigmoid(z[rows, :LRU_BLOCK] + bias_a)
            i = jax.nn.sigmoid(z[rows, LRU_BLOCK:] + bias_x)
            ag = jnp.exp(-LRU_C * r * sp)
            gap = 1.0 - ag * ag
            ug = jnp.where(gap > 0.0, gap * lax.rsqrt(gap), 0.0) * (i * xc)
            for dist in (1, 2, 4):
                ok = row8 >= dist
                a_prev = jnp.where(ok, pltpu.roll(ag, dist, axis=0), 1.0)
                u_prev = jnp.where(ok, pltpu.roll(ug, dist, axis=0), 0.0)
                ug = ag * u_prev + ug
                ag = ag * a_prev
            hg = ag * carry + ug
            carry = jnp.broadcast_to(hg[SUBLANES - 1:SUBLANES, :], (SUBLANES, LRU_BLOCK))
            outs.append(hg * jax.nn.gelu(y_l[rows, sl]))
        carry_ref[:, sl] = carry
        ol_ref[0, :, sl] = jnp.concatenate(outs, axis=0).astype(ol_ref.dtype)
        if n < len(other):
            other[n]()


def _proj_lru(x, g, w_in, w_bg, b_bg, conv_w, conv_b, wax, b_a, b_x, lam, tm, riders=()):
    b, s, d = x.shape
    row = pl.BlockSpec((1, tm, d), lambda bi, si: (bi, si, 0))
    bf = jax.ShapeDtypeStruct((b, s, d), BF16)
    tiles = s // tm
    r_in, r_out, r_shapes = _rider_specs(riders, b * tiles, lambda bi, si: bi * tiles + si)
    return pl.pallas_call(
        functools.partial(_proj_lru_kernel, tm=tm, n_riders=len(riders)),
        grid=(b, tiles),
        in_specs=[row, _const_spec((1, d)), _const_spec(w_in.shape), _const_spec(w_bg.shape),
                  _const_spec((1, 3 * d)), _const_spec((CONV_WIDTH, d)), _const_spec((1, d)),
                  _const_spec(wax.shape), _const_spec((1, d)), _const_spec((1, d)), _const_spec((1, d))] + r_in,
        out_specs=[row, row, row, row, pl.BlockSpec((1, tm, 3 * d), lambda bi, si: (bi, si, 0)), row] + r_out,
        out_shape=[bf, bf, bf, bf, jax.ShapeDtypeStruct((b, s, 3 * d), BF16), bf] + r_shapes,
        scratch_shapes=[pltpu.VMEM((SUBLANES, d), F32), pltpu.VMEM((SUBLANES, d), F32)],
        compiler_params=_params(("parallel", "arbitrary")),
        name="proj_lru",
    )(x, g, w_in, w_bg, b_bg, conv_w, conv_b, wax, b_a, b_x, lam, *riders)


def _diff_attn_kernel(slopes_ref, zero_ref, q_ref, k_ref, v_ref, lq1_ref, lk1_ref, lq2_ref, lk2_ref, hg_ref, o_ref,
                      vaug_ref, qm_ref, m_ref, acc_ref, s0_ref, s1_ref, p0_ref, p1_ref, a0_ref, a1_ref, *, tq):
    seq = q_ref.shape[1]
    dv = DA_V_DIM
    slope = slopes_ref[pl.program_id(1)] * LOG2E
    n_col_tiles = tq // LANES
    s_refs, p_refs, alpha_refs = (s0_ref, s1_ref), (p0_ref, p1_ref), (a0_ref, a1_ref)
    dyn0 = zero_ref[0]

    vaug_ref[:, :dv] = v_ref[0]
    vaug_ref[:, dv:] = jnp.ones((seq, dv), BF16)
    lam = (jnp.exp(jnp.sum(lq1_ref[...] * lk1_ref[...], axis=-1, keepdims=True))
           - jnp.exp(jnp.sum(lq2_ref[...] * lk2_ref[...], axis=-1, keepdims=True)) + LAM_INIT)

    lane = lax.broadcasted_iota(jnp.int32, (1, 2 * DA_QK_DIM), 1)
    col = lax.broadcasted_iota(jnp.int32, (1, tq), 1)
    rr = lax.broadcasted_iota(jnp.int32, (ROW_GROUP, LANES), 0)
    cc = lax.broadcasted_iota(jnp.int32, (ROW_GROUP, LANES), 1)
    pieces = [(qi, j) for qi in range(seq // tq) for j in range(qi + 1)]

    def load_queries(qi):
        q = q_ref[0, qi * tq:(qi + 1) * tq, :]
        zero = jnp.zeros_like(q)
        qm_ref[qi % 2, :tq, :] = jnp.where(lane < DA_QK_DIM, q, zero)
        qm_ref[qi % 2, tq:, :] = jnp.where(lane >= DA_QK_DIM, q, zero)

    def start_scores(n):
        qi, j = pieces[n]
        s_refs[n % 2][...] = _dot_nt(qm_ref[qi % 2], k_ref[0, j * tq:(j + 1) * tq, :])

    def softmax_pv(n):
        qi, j = pieces[n]
        buf = n % 2
        first, diagonal = j == 0, j == qi
        bias = slope * (col + (j - qi) * tq).astype(F32)
        for g in range(2 * tq // ROW_GROUP):
            r0 = g * ROW_GROUP
            rq = r0 % tq
            rows = pl.ds(pl.multiple_of(r0 + dyn0, ROW_GROUP), ROW_GROUP)
            tiles = []
            for c in range(n_col_tiles):
                c0 = c * LANES
                t = s_refs[buf][rows, c0:c0 + LANES] + bias[:, c0:c0 + LANES]
                if diagonal and c0 + LANES - 1 > rq:
                    t = jnp.where(rr + rq >= cc + c0, t, -jnp.inf)
                tiles.append(t)
            mx = jnp.max(functools.reduce(jnp.maximum, tiles), axis=-1, keepdims=True)
            if first:
                m_new = jnp.broadcast_to(mx, (ROW_GROUP, LANES))
            else:
                m_old = m_ref[rows, :]
                m_new = jnp.maximum(m_old, mx)
                alpha_refs[buf][rows, :] = jnp.exp2(m_old - m_new)
            m_ref[rows, :] = m_new
            for c in range(n_col_tiles):
                c0 = c * LANES
                p_refs[buf][rows, c0:c0 + LANES] = jnp.exp2(tiles[c] - m_new).astype(BF16)
        pv = _dot(p_refs[buf][...], vaug_ref[j * tq:(j + 1) * tq, :])
        if first:
            acc_ref[...] = pv
        else:
            alpha = alpha_refs[buf][...]
            acc_ref[:, :dv] = alpha * acc_ref[:, :dv] + pv[:, :dv]
            acc_ref[:, dv:] = alpha * acc_ref[:, dv:] + pv[:, dv:]

    def finalize(qi):
        o = (acc_ref[:tq, :dv] / acc_ref[:tq, dv:]) - lam * (acc_ref[tq:, :dv] / acc_ref[tq:, dv:])
        o_ref[0, qi * tq:(qi + 1) * tq, :] = (_rms(o, hg_ref[...]) * (1.0 - LAM_INIT)).astype(o_ref.dtype)

    load_queries(0)
    start_scores(0)
    for n, (qi, j) in enumerate(pieces):
        if n + 1 < len(pieces):
            if pieces[n + 1][0] != qi:
                load_queries(pieces[n + 1][0])
            start_scores(n + 1)
        softmax_pv(n)
        if j == qi:
            finalize(qi)


def _diff_attn(slopes, q, k, v, lq1, lk1, lq2, lk2, head_g, tq):
    b, s, d = q.shape
    dv = DA_V_DIM
    head = pl.BlockSpec((1, s, dv), lambda bi, hi: (bi, 0, hi))
    vec = lambda w: pl.BlockSpec((1, w), lambda bi, hi: (0, 0))
    return pl.pallas_call(
        functools.partial(_diff_attn_kernel, tq=tq),
        grid=(b, DA_HEADS),
        in_specs=[pl.BlockSpec(memory_space=pltpu.SMEM), pl.BlockSpec(memory_space=pltpu.SMEM), head, head, head,
                  vec(DA_QK_DIM), vec(DA_QK_DIM), vec(DA_QK_DIM), vec(DA_QK_DIM), vec(dv)],
        out_specs=head,
        out_shape=jax.ShapeDtypeStruct((b, s, d), BF16),
        scratch_shapes=[pltpu.VMEM((s, 2 * dv), BF16),
                        pltpu.VMEM((2, 2 * tq, dv), BF16),
                        pltpu.VMEM((2 * tq, LANES), F32),
                        pltpu.VMEM((2 * tq, 2 * dv), F32),
                        pltpu.VMEM((2 * tq, tq), F32), pltpu.VMEM((2 * tq, tq), F32),
                        pltpu.VMEM((2 * tq, tq), BF16), pltpu.VMEM((2 * tq, tq), BF16),
                        pltpu.VMEM((2 * tq, LANES), F32), pltpu.VMEM((2 * tq, LANES), F32)],
        compiler_params=_params(("parallel", "parallel")),
        name="diff_attn",
    )(slopes, jnp.zeros((1,), jnp.int32), q, k, v, lq1, lk1, lq2, lk2, head_g)


def _mem_kv_kernel(mem_ref, g_ref, w_ref, km_ref, vm_ref):
    mn = _rms(mem_ref[0], g_ref[...]).astype(BF16)
    d = D_MODEL
    km_ref[0] = _dot(mn, w_ref[:, :d]).astype(km_ref.dtype)
    vm_ref[0] = _dot(mn, w_ref[:, d:]).astype(vm_ref.dtype)


def _mem_kv(mem, g, w):
    b, m, d = mem.shape
    blk = pl.BlockSpec((1, m, d), lambda bi: (bi, 0, 0))
    out = jax.ShapeDtypeStruct((b, m, d), BF16)
    return pl.pallas_call(
        _mem_kv_kernel,
        grid=(b,),
        in_specs=[blk, _const_spec((1, d)), _const_spec(w.shape)],
        out_specs=[blk, blk],
        out_shape=[out, out],
        compiler_params=_params(("parallel",)),
        name="mem_kv",
    )(mem, g, w)


def _merge_kernel(x_ref, oda_ref, olru_ref, qc_ref, km_ref, vm_ref, gate_ref,
                  wda_ref, wlru_ref, wca_ref, wmix_ref, g_ref, *rest, n_riders):
    d = D_MODEL
    o_ref, oca_ref = rest[n_riders], rest[-1]
    _cast_riders(rest[:n_riders], rest[n_riders + 1:-1])
    qc = qc_ref[0]
    heads = [slice(hh * CA_HEAD_DIM, (hh + 1) * CA_HEAD_DIM) for hh in range(CA_HEADS)]
    scores = [_dot_nt(qc[:, sl], km_ref[0, :, sl]) for sl in heads]
    y_da = _dot(oda_ref[0], wda_ref[...])
    for sl, s in zip(heads, scores):
        p = jnp.exp(s - jnp.max(s, axis=-1, keepdims=True))
        o = _dot(p.astype(BF16), vm_ref[0, :, sl]) / jnp.sum(p, axis=-1, keepdims=True)
        oca_ref[:, sl] = o.astype(BF16)
    merged = (gate_ref[0, :, 0:d].astype(F32) * y_da
              + gate_ref[0, :, d:2 * d].astype(F32) * _dot(olru_ref[0], wlru_ref[...])
              + gate_ref[0, :, 2 * d:].astype(F32) * _dot(oca_ref[...], wca_ref[...]))
    out = _dot(merged.astype(BF16), wmix_ref[...])
    o_ref[0] = x_ref[0] + _rms(out, g_ref[...])


def _merge(x, o_da, o_lru, q_ca, km, vm, gates, w_da, w_lru, w_ca, w_mix, g, tm, riders=()):
    b, s, d = x.shape
    row = pl.BlockSpec((1, tm, d), lambda bi, si: (bi, si, 0))
    memspec = pl.BlockSpec((1, N_MEM, d), lambda bi, si: (bi, 0, 0))
    wspec = _const_spec((d, d))
    tiles = s // tm
    r_in, r_out, r_shapes = _rider_specs(riders, b * tiles, lambda bi, si: bi * tiles + si)
    return pl.pallas_call(
        functools.partial(_merge_kernel, n_riders=len(riders)),
        grid=(b, tiles),
        in_specs=[row, row, row, row, memspec, memspec,
                  pl.BlockSpec((1, tm, 3 * d), lambda bi, si: (bi, si, 0)),
                  wspec, wspec, wspec, wspec, _const_spec((1, d))] + r_in,
        out_specs=[row] + r_out,
        out_shape=[jax.ShapeDtypeStruct((b, s, d), F32)] + r_shapes,
        scratch_shapes=[pltpu.VMEM((tm, d), BF16)],
        compiler_params=_params(("parallel", "parallel")),
        name="merge",
    )(x, o_da, o_lru, q_ca, km, vm, gates, w_da, w_lru, w_ca, w_mix, g, *riders)


def kernel(x, mem, ffn1_pre_g, ffn1_w_gate, ffn1_w_up, ffn1_w_down, ffn1_post_g, mix_pre_g, w_in, da_lambda_q1, da_lambda_k1, da_lambda_q2, da_lambda_k2, da_head_g, w_da_out, lru_conv_w, lru_conv_b, lru_w_a, lru_b_a, lru_w_x, lru_b_x, lru_lambda, w_lru_out, mem_g, w_mem_kv, w_ca_out, w_branch_gate, b_branch_gate, w_mix_out, mix_post_g, ffn2_pre_g, ffn2_w_gate, ffn2_w_up, ffn2_w_down, ffn2_post_g):
    b, s, d = x.shape
    n = b * s
    bf = lambda w: w.astype(BF16)
    slopes = jnp.exp2(-8.0 * jnp.arange(1, DA_HEADS + 1, dtype=F32) / DA_HEADS)

    x2d = x.reshape(n, d)
    for l in range(ffn1_pre_g.shape[0]):
        x2d, w_in_b, w_bg_b, wd2_b = _ffn(
            x2d, ffn1_pre_g[l][None], bf(ffn1_w_gate[l]), bf(ffn1_w_up[l]), bf(ffn1_w_down[l]),
            ffn1_post_g[l][None], tm=FFN_ROWS, riders=(w_in[l], w_branch_gate[l], ffn2_w_down[l]))
        r3 = lambda t: t.reshape(b, s, -1)
        wax = bf(jnp.concatenate([lru_w_a[l], lru_w_x[l]], axis=-1))
        q_da, k_da, v_da, q_ca, gates, o_lru, w_da_b, w_lru_b, w_ca_b, w_mix_b, w_kv_b = _proj_lru(
            r3(x2d), mix_pre_g[l][None], w_in_b, w_bg_b, b_branch_gate[l][None],
            lru_conv_w[l], lru_conv_b[l][None], wax, lru_b_a[l][None], lru_b_x[l][None], lru_lambda[l][None],
            tm=MIX_ROWS, riders=(w_da_out[l], w_lru_out[l], w_ca_out[l], w_mix_out[l], w_mem_kv[l]))
        o_da = _diff_attn(slopes, q_da, k_da, v_da, da_lambda_q1[l][None], da_lambda_k1[l][None],
                          da_lambda_q2[l][None], da_lambda_k2[l][None], da_head_g[l][None], tq=ATTN_ROWS)
        km, vm = _mem_kv(mem, mem_g[l][None], w_kv_b)
        x3, wg2_b, wu2_b = _merge(
            r3(x2d), o_da, o_lru, q_ca, km, vm, gates, w_da_b, w_lru_b, w_ca_b, w_mix_b, mix_post_g[l][None],
            tm=MIX_ROWS, riders=(ffn2_w_gate[l], ffn2_w_up[l]))
        x2d, = _ffn(x3.reshape(n, d), ffn2_pre_g[l][None], wg2_b, wu2_b, wd2_b, ffn2_post_g[l][None],
                    tm=FFN_ROWS)
    return x2d.reshape(b, s, d)
```

```python
import functools
import math

import jax
import jax.numpy as jnp
from jax import lax
from jax.experimental import pallas as pl
from jax.experimental.pallas import tpu as pltpu

D_MODEL = 1024
N_MEM = 256
DA_HEADS = 8
DA_QK_DIM = 64
DA_V_DIM = 128
LRU_BLOCKS = 8
LRU_BLOCK = 128
CONV_WIDTH = 4
LRU_C = 8.0
CA_HEADS = 4
CA_HEAD_DIM = 256
D_FF = 2816
MACARON_WEIGHT = 0.5
EPS = 1e-6
LAM_INIT = 0.8 - 0.6 * math.exp(-0.3 * 0)
LOG2E = math.log2(math.e)

SUBLANES = 8
LANES = 128
ROW_GROUP = 16
FF_CHUNK = 256
FFN_ROWS = 1024
FFN_SPLIT = 2
MIX_ROWS = 512
ATTN_ROWS = 512
VMEM_LIMIT = 56 * 1024 * 1024

F32 = jnp.float32
BF16 = jnp.bfloat16


def _rms(xf, g):
    return xf * lax.rsqrt(jnp.mean(xf * xf, axis=-1, keepdims=True) + EPS) * g


def _dot(a, b):
    return jnp.dot(a, b, preferred_element_type=F32)


def _dot_nt(a, b):
    return lax.dot_general(a, b, (((1,), (1,)), ((), ())), preferred_element_type=F32)


def _const_spec(shape):
    nd = len(shape)
    return pl.BlockSpec(shape, lambda *_: (0,) * nd, pipeline_mode=pl.Buffered(1))


def _params(sem):
    return pltpu.CompilerParams(dimension_semantics=sem, vmem_limit_bytes=VMEM_LIMIT)


def _rider_specs(weights, n_steps, step_of):
    in_specs, out_specs, out_shapes = [], [], []
    for w in weights:
        rows, cols = w.shape
        spec = pl.BlockSpec((rows // n_steps, cols), lambda *g: (step_of(*g), 0))
        in_specs.append(spec)
        out_specs.append(spec)
        out_shapes.append(jax.ShapeDtypeStruct(w.shape, BF16))
    return in_specs, out_specs, out_shapes


def _cast_riders(in_refs, out_refs):
    for w_ref, o_ref in zip(in_refs, out_refs):
        o_ref[...] = w_ref[...].astype(BF16)


def _ffn_kernel(x_ref, pre_g_ref, wg_ref, wu_ref, wd_ref, post_g_ref, *rest, n_riders):
    o_ref, act_ref = rest[n_riders], rest[-1]
    _cast_riders(rest[:n_riders], rest[n_riders + 1:-1])
    half = x_ref.shape[0] // FFN_SPLIT
    for part in range(FFN_SPLIT):
        rows = slice(part * half, (part + 1) * half)
        x = x_ref[rows, :]
        rinv = lax.rsqrt(jnp.mean(x * x, axis=-1, keepdims=True) + EPS)
        xg = (x * pre_g_ref[...]).astype(BF16)
        for c in range(D_FF // FF_CHUNK):
            sl = slice(c * FF_CHUNK, (c + 1) * FF_CHUNK)
            a = rinv * _dot(xg, wg_ref[:, sl])
            b = rinv * _dot(xg, wu_ref[:, sl])
            act_ref[rows, sl] = (a * jax.nn.sigmoid(a) * b).astype(BF16)
        f = _dot(act_ref[rows, :], wd_ref[...])
        o_ref[rows, :] = x + MACARON_WEIGHT * _rms(f, post_g_ref[...])


def _ffn(x2d, pre_g, wg, wu, wd, post_g, tm, riders=()):
    n, d = x2d.shape
    row = pl.BlockSpec((tm, d), lambda i: (i, 0))
    r_in, r_out, r_shapes = _rider_specs(riders, n // tm, lambda i: i)
    return pl.pallas_call(
        functools.partial(_ffn_kernel, n_riders=len(riders)),
        grid=(n // tm,),
        in_specs=[row, _const_spec((1, d)), _const_spec((d, D_FF)), _const_spec((d, D_FF)),
                  _const_spec((D_FF, d)), _const_spec((1, d))] + r_in,
        out_specs=[row] + r_out,
        out_shape=[jax.ShapeDtypeStruct((n, d), F32)] + r_shapes,
        scratch_shapes=[pltpu.VMEM((tm, D_FF), BF16)],
        compiler_params=_params(("parallel",)),
        name="ffn",
    )(x2d, pre_g, wg, wu, wd, post_g, *riders)


def _proj_lru_kernel(x_ref, g_ref, w_in_ref, w_bg_ref, b_bg_ref, cw_ref, cb_ref, wax_ref, ba_ref, bx_ref,
                     lam_ref, *rest, tm, n_riders):
    d = D_MODEL
    q_ref, k_ref, v_ref, qc_ref, gate_ref, ol_ref = rest[n_riders:n_riders + 6]
    tail_ref, carry_ref = rest[-2:]
    _cast_riders(rest[:n_riders], rest[n_riders + 6:-2])

    @pl.when(pl.program_id(1) == 0)
    def _():
        tail_ref[...] = jnp.zeros(tail_ref.shape, F32)
        carry_ref[...] = jnp.zeros(carry_ref.shape, F32)

    h = _rms(x_ref[0], g_ref[...]).astype(BF16)
    x_l = _dot(h, w_in_ref[:, 3 * d:4 * d])
    y_l = _dot(h, w_in_ref[:, 4 * d:5 * d])

    def proj_out(j, o_ref, scale):
        p = _dot(h, w_in_ref[:, j * d:(j + 1) * d])
        o_ref[0] = (p if scale == 1.0 else p * scale).astype(o_ref.dtype)

    def gate_out(j):
        sl = slice(j * d, (j + 1) * d)
        z = _dot(h, w_bg_ref[:, sl]) + b_bg_ref[:, sl]
        gate_ref[0, :, sl] = jax.nn.sigmoid(z).astype(gate_ref.dtype)

    other = [functools.partial(proj_out, 0, q_ref, DA_QK_DIM ** -0.5 * LOG2E),
             functools.partial(proj_out, 1, k_ref, 1.0),
             functools.partial(proj_out, 2, v_ref, 1.0),
             functools.partial(proj_out, 5, qc_ref, CA_HEAD_DIM ** -0.5),
             functools.partial(gate_out, 0), functools.partial(gate_out, 1), functools.partial(gate_out, 2)]

    nlam = -lam_ref[...]
    softplus = jnp.maximum(nlam, 0.0) + jnp.log1p(jnp.exp(-jnp.abs(nlam)))
    row8 = lax.broadcasted_iota(jnp.int32, (SUBLANES, LRU_BLOCK), 0)
    for n in range(LRU_BLOCKS):
        sl = slice(n * LRU_BLOCK, (n + 1) * LRU_BLOCK)
        groups = [slice(g * SUBLANES, (g + 1) * SUBLANES) for g in range(tm // SUBLANES)]
        taps = [cw_ref[t:t + 1, sl] for t in range(CONV_WIDTH)]
        bias_c, bias_a, bias_x, sp = cb_ref[:, sl], ba_ref[:, sl], bx_ref[:, sl], softplus[:, sl]

        prev_rolled = [pltpu.roll(tail_ref[:, sl], back, axis=0) for back in range(1, CONV_WIDTH)]
        xcs = []
        for rows in groups:
            cur = x_l[rows, sl]
            xc = bias_c + cur * taps[CONV_WIDTH - 1]
            for back in range(1, CONV_WIDTH):
                rolled = pltpu.roll(cur, back, axis=0)
                xc = xc + jnp.where(row8 < back, prev_rolled[back - 1], rolled) * taps[CONV_WIDTH - 1 - back]
                prev_rolled[back - 1] = rolled
            xcs.append(xc)
        tail_ref[:, sl] = x_l[groups[-1], sl]

        z = _dot(jnp.concatenate(xcs, axis=0).astype(BF16), wax_ref[n])

        carry = carry_ref[:, sl]
        outs = []
        for rows, xc in zip(groups, xcs):
            r = jax.nn.sigmoid(z[rows, :LRU_BLOCK] + bias_a)
            i = jax.nn.sigmoid(z[rows, LRU_BLOCK:] + bias_x)
            ag = jnp.exp(-LRU_C * r * sp)
            gap = 1.0 - ag * ag
            ug = jnp.where(gap > 0.0, gap * lax.rsqrt(gap), 0.0) * (i * xc)
            for dist in (1, 2, 4):
                ok = row8 >= dist
                a_prev = jnp.where(ok, pltpu.roll(ag, dist, axis=0), 1.0)
                u_prev = jnp.where(ok, pltpu.roll(ug, dist, axis=0), 0.0)
                ug = ag * u_prev + ug
                ag = ag * a_prev
            hg = ag * carry + ug
            carry = jnp.broadcast_to(hg[SUBLANES - 1:SUBLANES, :], (SUBLANES, LRU_BLOCK))
            outs.append(hg * jax.nn.gelu(y_l[rows, sl]))
        carry_ref[:, sl] = carry
        ol_ref[0, :, sl] = jnp.concatenate(outs, axis=0).astype(ol_ref.dtype)
        if n < len(other):
            other[n]()


def _proj_lru(x, g, w_in, w_bg, b_bg, conv_w, conv_b, wax, b_a, b_x, lam, tm, riders=()):
    b, s, d = x.shape
    row = pl.BlockSpec((1, tm, d), lambda bi, si: (bi, si, 0))
    bf = jax.ShapeDtypeStruct((b, s, d), BF16)
    tiles = s // tm
    r_in, r_out, r_shapes = _rider_specs(riders, b * tiles, lambda bi, si: bi * tiles + si)
    return pl.pallas_call(
        functools.partial(_proj_lru_kernel, tm=tm, n_riders=len(riders)),
        grid=(b, tiles),
        in_specs=[row, _const_spec((1, d)), _const_spec(w_in.shape), _const_spec(w_bg.shape),
                  _const_spec((1, 3 * d)), _const_spec((CONV_WIDTH, d)), _const_spec((1, d)),
                  _const_spec(wax.shape), _const_spec((1, d)), _const_spec((1, d)), _const_spec((1, d))] + r_in,
        out_specs=[row, row, row, row, pl.BlockSpec((1, tm, 3 * d), lambda bi, si: (bi, si, 0)), row] + r_out,
        out_shape=[bf, bf, bf, bf, jax.ShapeDtypeStruct((b, s, 3 * d), BF16), bf] + r_shapes,
        scratch_shapes=[pltpu.VMEM((SUBLANES, d), F32), pltpu.VMEM((SUBLANES, d), F32)],
        compiler_params=_params(("parallel", "arbitrary")),
        name="proj_lru",
    )(x, g, w_in, w_bg, b_bg, conv_w, conv_b, wax, b_a, b_x, lam, *riders)


def _diff_attn_kernel(slopes_ref, zero_ref, q_ref, k_ref, v_ref, lq1_ref, lk1_ref, lq2_ref, lk2_ref, hg_ref, o_ref,
                      vaug_ref, qm_ref, m_ref, acc_ref, s0_ref, s1_ref, p0_ref, p1_ref, a0_ref, a1_ref, *, tq):
    seq = q_ref.shape[1]
    dv = DA_V_DIM
    slope = slopes_ref[pl.program_id(1)] * LOG2E
    n_col_tiles = tq // LANES
    s_refs, p_refs, alpha_refs = (s0_ref, s1_ref), (p0_ref, p1_ref), (a0_ref, a1_ref)
    dyn0 = zero_ref[0]

    vaug_ref[:, :dv] = v_ref[0]
    vaug_ref[:, dv:] = jnp.ones((seq, dv), BF16)
    lam = (jnp.exp(jnp.sum(lq1_ref[...] * lk1_ref[...], axis=-1, keepdims=True))
           - jnp.exp(jnp.sum(lq2_ref[...] * lk2_ref[...], axis=-1, keepdims=True)) + LAM_INIT)

    lane = lax.broadcasted_iota(jnp.int32, (1, 2 * DA_QK_DIM), 1)
    col = lax.broadcasted_iota(jnp.int32, (1, tq), 1)
    rr = lax.broadcasted_iota(jnp.int32, (ROW_GROUP, LANES), 0)
    cc = lax.broadcasted_iota(jnp.int32, (ROW_GROUP, LANES), 1)
    pieces = [(qi, j) for qi in range(seq // tq) for j in range(qi + 1)]

    def load_queries(qi):
        q = q_ref[0, qi * tq:(qi + 1) * tq, :]
        zero = jnp.zeros_like(q)
        qm_ref[qi % 2, :tq, :] = jnp.where(lane < DA_QK_DIM, q, zero)
        qm_ref[qi % 2, tq:, :] = jnp.where(lane >= DA_QK_DIM, q, zero)

    def start_scores(n):
        qi, j = pieces[n]
        s_refs[n % 2][...] = _dot_nt(qm_ref[qi % 2], k_ref[0, j * tq:(j + 1) * tq, :])

    def softmax_pv(n):
        qi, j = pieces[n]
        buf = n % 2
        first, diagonal = j == 0, j == qi
        bias = slope * (col + (j - qi) * tq).astype(F32)
        groups = [(pl.ds(pl.multiple_of(r0 + dyn0, ROW_GROUP), ROW_GROUP), r0 % tq)
                  for r0 in range(0, 2 * tq, ROW_GROUP)]

        def score_tile(rows, rq, c):
            c0 = c * LANES
            t = s_refs[buf][rows, c0:c0 + LANES] + bias[:, c0:c0 + LANES]
            if diagonal and c0 + LANES - 1 > rq:
                t = jnp.where(rr + rq >= cc + c0, t, -jnp.inf)
            return t

        for rows, rq in groups:
            tiles = [score_tile(rows, rq, c) for c in range(n_col_tiles)]
            mx = jnp.max(functools.reduce(jnp.maximum, tiles), axis=-1, keepdims=True)
            if first:
                m_new = jnp.broadcast_to(mx, (ROW_GROUP, LANES))
            else:
                m_old = m_ref[rows, :]
                m_new = jnp.maximum(m_old, mx)
                alpha_refs[buf][rows, :] = jnp.exp2(m_old - m_new)
            m_ref[rows, :] = m_new
        for rows, rq in groups:
            m_new = m_ref[rows, :]
            for c in range(n_col_tiles):
                p_refs[buf][rows, c * LANES:(c + 1) * LANES] = jnp.exp2(score_tile(rows, rq, c) - m_new).astype(BF16)
        pv = _dot(p_refs[buf][...], vaug_ref[j * tq:(j + 1) * tq, :])
        if first:
            acc_ref[...] = pv
        else:
            alpha = alpha_refs[buf][...]
            acc_ref[:, :dv] = alpha * acc_ref[:, :dv] + pv[:, :dv]
            acc_ref[:, dv:] = alpha * acc_ref[:, dv:] + pv[:, dv:]

    def finalize(qi):
        o = (acc_ref[:tq, :dv] / acc_ref[:tq, dv:]) - lam * (acc_ref[tq:, :dv] / acc_ref[tq:, dv:])
        o_ref[0, qi * tq:(qi + 1) * tq, :] = (_rms(o, hg_ref[...]) * (1.0 - LAM_INIT)).astype(o_ref.dtype)

    load_queries(0)
    start_scores(0)
    for n, (qi, j) in enumerate(pieces):
        if n + 1 < len(pieces):
            if pieces[n + 1][0] != qi:
                load_queries(pieces[n + 1][0])
            start_scores(n + 1)
        softmax_pv(n)
        if j == qi:
            finalize(qi)


def _diff_attn(slopes, q, k, v, lq1, lk1, lq2, lk2, head_g, tq):
    b, s, d = q.shape
    dv = DA_V_DIM
    head = pl.BlockSpec((1, s, dv), lambda bi, hi: (bi, 0, hi))
    vec = lambda w: pl.BlockSpec((1, w), lambda bi, hi: (0, 0))
    return pl.pallas_call(
        functools.partial(_diff_attn_kernel, tq=tq),
        grid=(b, DA_HEADS),
        in_specs=[pl.BlockSpec(memory_space=pltpu.SMEM), pl.BlockSpec(memory_space=pltpu.SMEM), head, head, head,
                  vec(DA_QK_DIM), vec(DA_QK_DIM), vec(DA_QK_DIM), vec(DA_QK_DIM), vec(dv)],
        out_specs=head,
        out_shape=jax.ShapeDtypeStruct((b, s, d), BF16),
        scratch_shapes=[pltpu.VMEM((s, 2 * dv), BF16),
                        pltpu.VMEM((2, 2 * tq, dv), BF16),
                        pltpu.VMEM((2 * tq, LANES), F32),
                        pltpu.VMEM((2 * tq, 2 * dv), F32),
                        pltpu.VMEM((2 * tq, tq), F32), pltpu.VMEM((2 * tq, tq), F32),
                        pltpu.VMEM((2 * tq, tq), BF16), pltpu.VMEM((2 * tq, tq), BF16),
                        pltpu.VMEM((2 * tq, LANES), F32), pltpu.VMEM((2 * tq, LANES), F32)],
        compiler_params=_params(("parallel", "parallel")),
        name="diff_attn",
    )(slopes, jnp.zeros((1,), jnp.int32), q, k, v, lq1, lk1, lq2, lk2, head_g)


def _mem_kv_kernel(mem_ref, g_ref, w_ref, km_ref, vm_ref):
    mn = _rms(mem_ref[0], g_ref[...]).astype(BF16)
    d = D_MODEL
    km_ref[0] = _dot(mn, w_ref[:, :d]).astype(km_ref.dtype)
    vm_ref[0] = _dot(mn, w_ref[:, d:]).astype(vm_ref.dtype)


def _mem_kv(mem, g, w):
    b, m, d = mem.shape
    blk = pl.BlockSpec((1, m, d), lambda bi: (bi, 0, 0))
    out = jax.ShapeDtypeStruct((b, m, d), BF16)
    return pl.pallas_call(
        _mem_kv_kernel,
        grid=(b,),
        in_specs=[blk, _const_spec((1, d)), _const_spec(w.shape)],
        out_specs=[blk, blk],
        out_shape=[out, out],
        compiler_params=_params(("parallel",)),
        name="mem_kv",
    )(mem, g, w)


def _merge_kernel(x_ref, oda_ref, olru_ref, qc_ref, km_ref, vm_ref, gate_ref,
                  wda_ref, wlru_ref, wca_ref, wmix_ref, g_ref, *rest, n_riders):
    d = D_MODEL
    o_ref, oca_ref = rest[n_riders], rest[-1]
    _cast_riders(rest[:n_riders], rest[n_riders + 1:-1])
    qc = qc_ref[0]
    heads = [slice(hh * CA_HEAD_DIM, (hh + 1) * CA_HEAD_DIM) for hh in range(CA_HEADS)]
    scores = [_dot_nt(qc[:, sl], km_ref[0, :, sl]) for sl in heads]
    y_da = _dot(oda_ref[0], wda_ref[...])
    for sl, s in zip(heads, scores):
        p = jnp.exp(s - jnp.max(s, axis=-1, keepdims=True))
        o = _dot(p.astype(BF16), vm_ref[0, :, sl]) / jnp.sum(p, axis=-1, keepdims=True)
        oca_ref[:, sl] = o.astype(BF16)
    merged = (gate_ref[0, :, 0:d].astype(F32) * y_da
              + gate_ref[0, :, d:2 * d].astype(F32) * _dot(olru_ref[0], wlru_ref[...])
              + gate_ref[0, :, 2 * d:].astype(F32) * _dot(oca_ref[...], wca_ref[...]))
    out = _dot(merged.astype(BF16), wmix_ref[...])
    o_ref[0] = x_ref[0] + _rms(out, g_ref[...])


def _merge(x, o_da, o_lru, q_ca, km, vm, gates, w_da, w_lru, w_ca, w_mix, g, tm, riders=()):
    b, s, d = x.shape
    row = pl.BlockSpec((1, tm, d), lambda bi, si: (bi, si, 0))
    memspec = pl.BlockSpec((1, N_MEM, d), lambda bi, si: (bi, 0, 0))
    wspec = _const_spec((d, d))
    tiles = s // tm
    r_in, r_out, r_shapes = _rider_specs(riders, b * tiles, lambda bi, si: bi * tiles + si)
    return pl.pallas_call(
        functools.partial(_merge_kernel, n_riders=len(riders)),
        grid=(b, tiles),
        in_specs=[row, row, row, row, memspec, memspec,
                  pl.BlockSpec((1, tm, 3 * d), lambda bi, si: (bi, si, 0)),
                  wspec, wspec, wspec, wspec, _const_spec((1, d))] + r_in,
        out_specs=[row] + r_out,
        out_shape=[jax.ShapeDtypeStruct((b, s, d), F32)] + r_shapes,
        scratch_shapes=[pltpu.VMEM((tm, d), BF16)],
        compiler_params=_params(("parallel", "parallel")),
        name="merge",
    )(x, o_da, o_lru, q_ca, km, vm, gates, w_da, w_lru, w_ca, w_mix, g, *riders)


def kernel(x, mem, ffn1_pre_g, ffn1_w_gate, ffn1_w_up, ffn1_w_down, ffn1_post_g, mix_pre_g, w_in, da_lambda_q1, da_lambda_k1, da_lambda_q2, da_lambda_k2, da_head_g, w_da_out, lru_conv_w, lru_conv_b, lru_w_a, lru_b_a, lru_w_x, lru_b_x, lru_lambda, w_lru_out, mem_g, w_mem_kv, w_ca_out, w_branch_gate, b_branch_gate, w_mix_out, mix_post_g, ffn2_pre_g, ffn2_w_gate, ffn2_w_up, ffn2_w_down, ffn2_post_g):
    b, s, d = x.shape
    n = b * s
    bf = lambda w: w.astype(BF16)
    slopes = jnp.exp2(-8.0 * jnp.arange(1, DA_HEADS + 1, dtype=F32) / DA_HEADS)

    x2d = x.reshape(n, d)
    for l in range(ffn1_pre_g.shape[0]):
        x2d, w_in_b, w_bg_b, wd2_b = _ffn(
            x2d, ffn1_pre_g[l][None], bf(ffn1_w_gate[l]), bf(ffn1_w_up[l]), bf(ffn1_w_down[l]),
            ffn1_post_g[l][None], tm=FFN_ROWS, riders=(w_in[l], w_branch_gate[l], ffn2_w_down[l]))
        r3 = lambda t: t.reshape(b, s, -1)
        wax = bf(jnp.concatenate([lru_w_a[l], lru_w_x[l]], axis=-1))
        q_da, k_da, v_da, q_ca, gates, o_lru, w_da_b, w_lru_b, w_ca_b, w_mix_b, w_kv_b = _proj_lru(
            r3(x2d), mix_pre_g[l][None], w_in_b, w_bg_b, b_branch_gate[l][None],
            lru_conv_w[l], lru_conv_b[l][None], wax, lru_b_a[l][None], lru_b_x[l][None], lru_lambda[l][None],
            tm=MIX_ROWS, riders=(w_da_out[l], w_lru_out[l], w_ca_out[l], w_mix_out[l], w_mem_kv[l]))
        o_da = _diff_attn(slopes, q_da, k_da, v_da, da_lambda_q1[l][None], da_lambda_k1[l][None],
                          da_lambda_q2[l][None], da_lambda_k2[l][None], da_head_g[l][None], tq=ATTN_ROWS)
        km, vm = _mem_kv(mem, mem_g[l][None], w_kv_b)
        x3, wg2_b, wu2_b = _merge(
            r3(x2d), o_da, o_lru, q_ca, km, vm, gates, w_da_b, w_lru_b, w_ca_b, w_mix_b, mix_post_g[l][None],
            tm=MIX_ROWS, riders=(ffn2_w_gate[l], ffn2_w_up[l]))
        x2d, = _ffn(x3.reshape(n, d), ffn2_pre_g[l][None], wg2_b, wu2_b, wd2_b, ffn2_post_g[l][None],
                    tm=FFN_ROWS)
    return x2d.reshape(b, s, d)
```

```python
import functools
import math

import jax
import jax.numpy as jnp
from jax import lax
from jax.experimental import pallas as pl
from jax.experimental.pallas import tpu as pltpu

D_MODEL = 1024
N_MEM = 256
DA_HEADS = 8
DA_QK_DIM = 64
DA_V_DIM = 128
LRU_BLOCKS = 8
LRU_BLOCK = 128
CONV_WIDTH = 4
LRU_C = 8.0
CA_HEADS = 4
CA_HEAD_DIM = 256
D_FF = 2816
MACARON_WEIGHT = 0.5
EPS = 1e-6
LAM_INIT = 0.8 - 0.6 * math.exp(-0.3 * 0)
LOG2E = math.log2(math.e)

SUBLANES = 8
LANES = 128
ROW_GROUP = 16
FF_CHUNK = 256
FFN_ROWS = 1024
FFN_SPLIT = 2
MIX_ROWS = 512
ATTN_ROWS = 512
VMEM_LIMIT = 56 * 1024 * 1024

F32 = jnp.float32
BF16 = jnp.bfloat16


def _rms(xf, g):
    return xf * lax.rsqrt(jnp.mean(xf * xf, axis=-1, keepdims=True) + EPS) * g


def _dot(a, b):
    return jnp.dot(a, b, preferred_element_type=F32)


def _dot_nt(a, b):
    return lax.dot_general(a, b, (((1,), (1,)), ((), ())), preferred_element_type=F32)


def _const_spec(shape):
    nd = len(shape)
    return pl.BlockSpec(shape, lambda *_: (0,) * nd, pipeline_mode=pl.Buffered(1))


def _params(sem):
    return pltpu.CompilerParams(dimension_semantics=sem, vmem_limit_bytes=VMEM_LIMIT)


def _rider_specs(weights, n_steps, step_of):
    in_specs, out_specs, out_shapes = [], [], []
    for w in weights:
        rows, cols = w.shape
        spec = pl.BlockSpec((rows // n_steps, cols), lambda *g: (step_of(*g), 0))
        in_specs.append(spec)
        out_specs.append(spec)
        out_shapes.append(jax.ShapeDtypeStruct(w.shape, BF16))
    return in_specs, out_specs, out_shapes


def _cast_riders(in_refs, out_refs):
    for w_ref, o_ref in zip(in_refs, out_refs):
        o_ref[...] = w_ref[...].astype(BF16)


def _ffn_kernel(x_ref, pre_g_ref, wg_ref, wu_ref, wd_ref, post_g_ref, *rest, n_riders):
    o_ref, act_ref = rest[n_riders], rest[-1]
    _cast_riders(rest[:n_riders], rest[n_riders + 1:-1])
    half = x_ref.shape[0] // FFN_SPLIT
    for part in range(FFN_SPLIT):
        rows = slice(part * half, (part + 1) * half)
        x = x_ref[rows, :]
        rinv = lax.rsqrt(jnp.mean(x * x, axis=-1, keepdims=True) + EPS)
        xg = (x * pre_g_ref[...]).astype(BF16)
        for c in range(D_FF // FF_CHUNK):
            sl = slice(c * FF_CHUNK, (c + 1) * FF_CHUNK)
            a = rinv * _dot(xg, wg_ref[:, sl])
            b = rinv * _dot(xg, wu_ref[:, sl])
            act_ref[rows, sl] = (a * jax.nn.sigmoid(a) * b).astype(BF16)
        f = _dot(act_ref[rows, :], wd_ref[...])
        o_ref[rows, :] = x + MACARON_WEIGHT * _rms(f, post_g_ref[...])


def _ffn(x2d, pre_g, wg, wu, wd, post_g, tm, riders=()):
    n, d = x2d.shape
    row = pl.BlockSpec((tm, d), lambda i: (i, 0))
    r_in, r_out, r_shapes = _rider_specs(riders, n // tm, lambda i: i)
    return pl.pallas_call(
        functools.partial(_ffn_kernel, n_riders=len(riders)),
        grid=(n // tm,),
        in_specs=[row, _const_spec((1, d)), _const_spec((d, D_FF)), _const_spec((d, D_FF)),
                  _const_spec((D_FF, d)), _const_spec((1, d))] + r_in,
        out_specs=[row] + r_out,
        out_shape=[jax.ShapeDtypeStruct((n, d), F32)] + r_shapes,
        scratch_shapes=[pltpu.VMEM((tm, D_FF), BF16)],
        compiler_params=_params(("parallel",)),
        name="ffn",
    )(x2d, pre_g, wg, wu, wd, post_g, *riders)


def _gelu_tanh(y):
    k = -2.0 * math.sqrt(2.0 / math.pi) * LOG2E
    return y / (1.0 + jnp.exp2(y * (k + (0.044715 * k) * (y * y))))


def _proj_lru_kernel(x_ref, g_ref, w_in_ref, w_bg_ref, b_bg_ref, cw_ref, cb_ref, wax_ref, ba_ref, bx_ref,
                     lam_ref, *rest, tm, n_riders):
    d = D_MODEL
    q_ref, k_ref, v_ref, qc_ref, gate_ref, ol_ref = rest[n_riders:n_riders + 6]
    tail_ref, carry_ref = rest[-2:]
    _cast_riders(rest[:n_riders], rest[n_riders + 6:-2])

    @pl.when(pl.program_id(1) == 0)
    def _():
        tail_ref[...] = jnp.zeros(tail_ref.shape, F32)
        carry_ref[...] = jnp.zeros(carry_ref.shape, F32)

    h = _rms(x_ref[0], g_ref[...]).astype(BF16)
    x_l = _dot(h, w_in_ref[:, 3 * d:4 * d])
    y_l = _dot(h, w_in_ref[:, 4 * d:5 * d])

    def proj_out(j, o_ref, scale):
        p = _dot(h, w_in_ref[:, j * d:(j + 1) * d])
        o_ref[0] = (p if scale == 1.0 else p * scale).astype(o_ref.dtype)

    def gate_out(j):
        sl = slice(j * d, (j + 1) * d)
        z = _dot(h, w_bg_ref[:, sl]) + b_bg_ref[:, sl]
        gate_ref[0, :, sl] = jax.nn.sigmoid(z).astype(gate_ref.dtype)

    other = [functools.partial(proj_out, 0, q_ref, DA_QK_DIM ** -0.5 * LOG2E),
             functools.partial(proj_out, 1, k_ref, 1.0),
             functools.partial(proj_out, 2, v_ref, 1.0),
             functools.partial(proj_out, 5, qc_ref, CA_HEAD_DIM ** -0.5),
             functools.partial(gate_out, 0), functools.partial(gate_out, 1), functools.partial(gate_out, 2)]

    nlam = -lam_ref[...]
    softplus = jnp.maximum(nlam, 0.0) + jnp.log1p(jnp.exp(-jnp.abs(nlam)))
    decay_rate = (-LRU_C * LOG2E) * softplus
    row8 = lax.broadcasted_iota(jnp.int32, (SUBLANES, LRU_BLOCK), 0)
    for n in range(LRU_BLOCKS):
        sl = slice(n * LRU_BLOCK, (n + 1) * LRU_BLOCK)
        groups = [slice(g * SUBLANES, (g + 1) * SUBLANES) for g in range(tm // SUBLANES)]
        taps = [cw_ref[t:t + 1, sl] for t in range(CONV_WIDTH)]
        bias_c, bias_a, bias_x, rate = cb_ref[:, sl], ba_ref[:, sl], bx_ref[:, sl], decay_rate[:, sl]

        prev_rolled = [pltpu.roll(tail_ref[:, sl], back, axis=0) for back in range(1, CONV_WIDTH)]
        xcs = []
        for rows in groups:
            cur = x_l[rows, sl]
            xc = bias_c + cur * taps[CONV_WIDTH - 1]
            for back in range(1, CONV_WIDTH):
                rolled = pltpu.roll(cur, back, axis=0)
                xc = xc + jnp.where(row8 < back, prev_rolled[back - 1], rolled) * taps[CONV_WIDTH - 1 - back]
                prev_rolled[back - 1] = rolled
            xcs.append(xc)
        tail_ref[:, sl] = x_l[groups[-1], sl]

        z = _dot(jnp.concatenate(xcs, axis=0).astype(BF16), wax_ref[n])

        carry = carry_ref[:, sl]
        outs = []
        for rows, xc in zip(groups, xcs):
            r = jax.nn.sigmoid(z[rows, :LRU_BLOCK] + bias_a)
            i = jax.nn.sigmoid(z[rows, LRU_BLOCK:] + bias_x)
            ag = jnp.exp2(r * rate)
            gap = 1.0 - ag * ag
            ug = jnp.where(gap > 0.0, gap * lax.rsqrt(gap), 0.0) * (i * xc)
            for dist in (1, 2, 4):
                ok = row8 >= dist
                a_prev = jnp.where(ok, pltpu.roll(ag, dist, axis=0), 1.0)
                u_prev = jnp.where(ok, pltpu.roll(ug, dist, axis=0), 0.0)
                ug = ag * u_prev + ug
                ag = ag * a_prev
            hg = ag * carry + ug
            carry = jnp.broadcast_to(hg[SUBLANES - 1:SUBLANES, :], (SUBLANES, LRU_BLOCK))
            outs.append(hg * _gelu_tanh(y_l[rows, sl]))
        carry_ref[:, sl] = carry
        ol_ref[0, :, sl] = jnp.concatenate(outs, axis=0).astype(ol_ref.dtype)
        if n < len(other):
            other[n]()


def _proj_lru(x, g, w_in, w_bg, b_bg, conv_w, conv_b, wax, b_a, b_x, lam, tm, riders=()):
    b, s, d = x.shape
    row = pl.BlockSpec((1, tm, d), lambda bi, si: (bi, si, 0))
    bf = jax.ShapeDtypeStruct((b, s, d), BF16)
    tiles = s // tm
    r_in, r_out, r_shapes = _rider_specs(riders, b * tiles, lambda bi, si: bi * tiles + si)
    return pl.pallas_call(
        functools.partial(_proj_lru_kernel, tm=tm, n_riders=len(riders)),
        grid=(b, tiles),
        in_specs=[row, _const_spec((1, d)), _const_spec(w_in.shape), _const_spec(w_bg.shape),
                  _const_spec((1, 3 * d)), _const_spec((CONV_WIDTH, d)), _const_spec((1, d)),
                  _const_spec(wax.shape), _const_spec((1, d)), _const_spec((1, d)), _const_spec((1, d))] + r_in,
        out_specs=[row, row, row, row, pl.BlockSpec((1, tm, 3 * d), lambda bi, si: (bi, si, 0)), row] + r_out,
        out_shape=[bf, bf, bf, bf, jax.ShapeDtypeStruct((b, s, 3 * d), BF16), bf] + r_shapes,
        scratch_shapes=[pltpu.VMEM((SUBLANES, d), F32), pltpu.VMEM((SUBLANES, d), F32)],
        compiler_params=_params(("parallel", "arbitrary")),
        name="proj_lru",
    )(x, g, w_in, w_bg, b_bg, conv_w, conv_b, wax, b_a, b_x, lam, *riders)


def _diff_attn_kernel(slopes_ref, zero_ref, q_ref, k_ref, v_ref, lq1_ref, lk1_ref, lq2_ref, lk2_ref, hg_ref, o_ref,
                      vaug_ref, qm_ref, m_ref, acc_ref, s0_ref, s1_ref, p0_ref, p1_ref, a0_ref, a1_ref, *, tq):
    seq = q_ref.shape[1]
    dv = DA_V_DIM
    slope = slopes_ref[pl.program_id(1)] * LOG2E
    n_col_tiles = tq // LANES
    half = tq // 2
    s_refs, p_refs, alpha_refs = (s0_ref, s1_ref), (p0_ref, p1_ref), (a0_ref, a1_ref)
    dyn0 = zero_ref[0]

    vaug_ref[:, :dv] = v_ref[0]
    vaug_ref[:, dv:] = jnp.ones((seq, dv), BF16)
    lam = (jnp.exp(jnp.sum(lq1_ref[...] * lk1_ref[...], axis=-1, keepdims=True))
           - jnp.exp(jnp.sum(lq2_ref[...] * lk2_ref[...], axis=-1, keepdims=True)) + LAM_INIT)

    lane = lax.broadcasted_iota(jnp.int32, (1, 2 * DA_QK_DIM), 1)
    col = lax.broadcasted_iota(jnp.int32, (1, tq), 1)
    rr = lax.broadcasted_iota(jnp.int32, (ROW_GROUP, LANES), 0)
    cc = lax.broadcasted_iota(jnp.int32, (ROW_GROUP, LANES), 1)
    pieces = [(qi, j) for qi in range(seq // tq) for j in range(qi + 1)]

    def load_queries(qi):
        q = q_ref[0, qi * tq:(qi + 1) * tq, :]
        zero = jnp.zeros_like(q)
        for part in range(2):
            lo = 2 * part * half
            qm_ref[qi % 2, lo:lo + half, :] = jnp.where(lane < DA_QK_DIM, q, zero)[part * half:(part + 1) * half]
            qm_ref[qi % 2, lo + half:lo + tq, :] = jnp.where(lane >= DA_QK_DIM, q, zero)[part * half:(part + 1) * half]

    def start_scores(n):
        qi, j = pieces[n]
        k0 = j * tq
        if j == qi:
            s_refs[n % 2][:tq, :half] = _dot_nt(qm_ref[qi % 2, :tq, :], k_ref[0, k0:k0 + half, :])
            s_refs[n % 2][tq:, :] = _dot_nt(qm_ref[qi % 2, tq:, :], k_ref[0, k0:k0 + tq, :])
        else:
            s_refs[n % 2][...] = _dot_nt(qm_ref[qi % 2], k_ref[0, k0:k0 + tq, :])

    def softmax_pv(n):
        qi, j = pieces[n]
        buf = n % 2
        first, diagonal = j == 0, j == qi
        bias = slope * (col + (j - qi) * tq).astype(F32)
        groups = [(pl.ds(pl.multiple_of(r0 + dyn0, ROW_GROUP), ROW_GROUP), (r0 // tq) * half + r0 % half,
                   n_col_tiles // 2 if diagonal and r0 < tq else n_col_tiles)
                  for r0 in range(0, 2 * tq, ROW_GROUP)]

        def score_tile(rows, rq, c):
            c0 = c * LANES
            t = s_refs[buf][rows, c0:c0 + LANES] + bias[:, c0:c0 + LANES]
            if diagonal and c0 + LANES - 1 > rq:
                t = jnp.where(rr + rq >= cc + c0, t, -jnp.inf)
            return t

        for rows, rq, n_cols in groups:
            tiles = [score_tile(rows, rq, c) for c in range(n_cols)]
            mx = jnp.max(functools.reduce(jnp.maximum, tiles), axis=-1, keepdims=True)
            if first:
                m_new = jnp.broadcast_to(mx, (ROW_GROUP, LANES))
            else:
                m_old = m_ref[rows, :]
                m_new = jnp.maximum(m_old, mx)
                alpha_refs[buf][rows, :] = jnp.exp2(m_old - m_new)
            m_ref[rows, :] = m_new
        for rows, rq, n_cols in groups:
            m_new = m_ref[rows, :]
            for c in range(n_cols):
                p_refs[buf][rows, c * LANES:(c + 1) * LANES] = jnp.exp2(score_tile(rows, rq, c) - m_new).astype(BF16)
        k0 = j * tq
        if diagonal:
            products = [(slice(0, tq), _dot(p_refs[buf][:tq, :half], vaug_ref[k0:k0 + half, :])),
                        (slice(tq, 2 * tq), _dot(p_refs[buf][tq:, :], vaug_ref[k0:k0 + tq, :]))]
        else:
            products = [(slice(0, 2 * tq), _dot(p_refs[buf][...], vaug_ref[k0:k0 + tq, :]))]
        for part_rows, pv in products:
            if first:
                acc_ref[part_rows, :] = pv
            else:
                alpha = alpha_refs[buf][part_rows, :]
                acc_ref[part_rows, :dv] = alpha * acc_ref[part_rows, :dv] + pv[:, :dv]
                acc_ref[part_rows, dv:] = alpha * acc_ref[part_rows, dv:] + pv[:, dv:]

    def finalize(qi):
        for part in range(2):
            m1 = slice(2 * part * half, (2 * part + 1) * half)
            m2 = slice((2 * part + 1) * half, (2 * part + 2) * half)
            o = (acc_ref[m1, :dv] / acc_ref[m1, dv:]) - lam * (acc_ref[m2, :dv] / acc_ref[m2, dv:])
            out_rows = slice(qi * tq + part * half, qi * tq + (part + 1) * half)
            o_ref[0, out_rows, :] = (_rms(o, hg_ref[...]) * (1.0 - LAM_INIT)).astype(o_ref.dtype)

    load_queries(0)
    start_scores(0)
    for n, (qi, j) in enumerate(pieces):
        if n + 1 < len(pieces):
            if pieces[n + 1][0] != qi:
                load_queries(pieces[n + 1][0])
            start_scores(n + 1)
        softmax_pv(n)
        if j == qi:
            finalize(qi)


def _diff_attn(slopes, q, k, v, lq1, lk1, lq2, lk2, head_g, tq):
    b, s, d = q.shape
    dv = DA_V_DIM
    head = pl.BlockSpec((1, s, dv), lambda bi, hi: (bi, 0, hi))
    vec = lambda w: pl.BlockSpec((1, w), lambda bi, hi: (0, 0))
    return pl.pallas_call(
        functools.partial(_diff_attn_kernel, tq=tq),
        grid=(b, DA_HEADS),
        in_specs=[pl.BlockSpec(memory_space=pltpu.SMEM), pl.BlockSpec(memory_space=pltpu.SMEM), head, head, head,
                  vec(DA_QK_DIM), vec(DA_QK_DIM), vec(DA_QK_DIM), vec(DA_QK_DIM), vec(dv)],
        out_specs=head,
        out_shape=jax.ShapeDtypeStruct((b, s, d), BF16),
        scratch_shapes=[pltpu.VMEM((s, 2 * dv), BF16),
                        pltpu.VMEM((2, 2 * tq, dv), BF16),
                        pltpu.VMEM((2 * tq, LANES), F32),
                        pltpu.VMEM((2 * tq, 2 * dv), F32),
                        pltpu.VMEM((2 * tq, tq), F32), pltpu.VMEM((2 * tq, tq), F32),
                        pltpu.VMEM((2 * tq, tq), BF16), pltpu.VMEM((2 * tq, tq), BF16),
                        pltpu.VMEM((2 * tq, LANES), F32), pltpu.VMEM((2 * tq, LANES), F32)],
        compiler_params=_params(("parallel", "parallel")),
        name="diff_attn",
    )(slopes, jnp.zeros((1,), jnp.int32), q, k, v, lq1, lk1, lq2, lk2, head_g)


def _mem_kv_kernel(mem_ref, g_ref, w_ref, km_ref, vm_ref):
    mn = _rms(mem_ref[0], g_ref[...]).astype(BF16)
    d = D_MODEL
    km_ref[0] = _dot(mn, w_ref[:, :d]).astype(km_ref.dtype)
    vm_ref[0] = _dot(mn, w_ref[:, d:]).astype(vm_ref.dtype)


def _mem_kv(mem, g, w):
    b, m, d = mem.shape
    blk = pl.BlockSpec((1, m, d), lambda bi: (bi, 0, 0))
    out = jax.ShapeDtypeStruct((b, m, d), BF16)
    return pl.pallas_call(
        _mem_kv_kernel,
        grid=(b,),
        in_specs=[blk, _const_spec((1, d)), _const_spec(w.shape)],
        out_specs=[blk, blk],
        out_shape=[out, out],
        compiler_params=_params(("parallel",)),
        name="mem_kv",
    )(mem, g, w)


def _merge_kernel(x_ref, oda_ref, olru_ref, qc_ref, km_ref, vm_ref, gate_ref,
                  wda_ref, wlru_ref, wca_ref, wmix_ref, g_ref, *rest, n_riders):
    d = D_MODEL
    o_ref, oca_ref = rest[n_riders], rest[-1]
    _cast_riders(rest[:n_riders], rest[n_riders + 1:-1])
    qc = qc_ref[0]
    heads = [slice(hh * CA_HEAD_DIM, (hh + 1) * CA_HEAD_DIM) for hh in range(CA_HEADS)]
    scores = [_dot_nt(qc[:, sl], km_ref[0, :, sl]) for sl in heads]
    y_da = _dot(oda_ref[0], wda_ref[...])
    for sl, s in zip(heads, scores):
        p = jnp.exp(s - jnp.max(s, axis=-1, keepdims=True))
        o = _dot(p.astype(BF16), vm_ref[0, :, sl]) / jnp.sum(p, axis=-1, keepdims=True)
        oca_ref[:, sl] = o.astype(BF16)
    merged = (gate_ref[0, :, 0:d].astype(F32) * y_da
              + gate_ref[0, :, d:2 * d].astype(F32) * _dot(olru_ref[0], wlru_ref[...])
              + gate_ref[0, :, 2 * d:].astype(F32) * _dot(oca_ref[...], wca_ref[...]))
    out = _dot(merged.astype(BF16), wmix_ref[...])
    o_ref[0] = x_ref[0] + _rms(out, g_ref[...])


def _merge(x, o_da, o_lru, q_ca, km, vm, gates, w_da, w_lru, w_ca, w_mix, g, tm, riders=()):
    b, s, d = x.shape
    row = pl.BlockSpec((1, tm, d), lambda bi, si: (bi, si, 0))
    memspec = pl.BlockSpec((1, N_MEM, d), lambda bi, si: (bi, 0, 0))
    wspec = _const_spec((d, d))
    tiles = s // tm
    r_in, r_out, r_shapes = _rider_specs(riders, b * tiles, lambda bi, si: bi * tiles + si)
    return pl.pallas_call(
        functools.partial(_merge_kernel, n_riders=len(riders)),
        grid=(b, tiles),
        in_specs=[row, row, row, row, memspec, memspec,
                  pl.BlockSpec((1, tm, 3 * d), lambda bi, si: (bi, si, 0)),
                  wspec, wspec, wspec, wspec, _const_spec((1, d))] + r_in,
        out_specs=[row] + r_out,
        out_shape=[jax.ShapeDtypeStruct((b, s, d), F32)] + r_shapes,
        scratch_shapes=[pltpu.VMEM((tm, d), BF16)],
        compiler_params=_params(("parallel", "parallel")),
        name="merge",
    )(x, o_da, o_lru, q_ca, km, vm, gates, w_da, w_lru, w_ca, w_mix, g, *riders)


def kernel(x, mem, ffn1_pre_g, ffn1_w_gate, ffn1_w_up, ffn1_w_down, ffn1_post_g, mix_pre_g, w_in, da_lambda_q1, da_lambda_k1, da_lambda_q2, da_lambda_k2, da_head_g, w_da_out, lru_conv_w, lru_conv_b, lru_w_a, lru_b_a, lru_w_x, lru_b_x, lru_lambda, w_lru_out, mem_g, w_mem_kv, w_ca_out, w_branch_gate, b_branch_gate, w_mix_out, mix_post_g, ffn2_pre_g, ffn2_w_gate, ffn2_w_up, ffn2_w_down, ffn2_post_g):
    b, s, d = x.shape
    n = b * s
    bf = lambda w: w.astype(BF16)
    slopes = jnp.exp2(-8.0 * jnp.arange(1, DA_HEADS + 1, dtype=F32) / DA_HEADS)

    x2d = x.reshape(n, d)
    for l in range(ffn1_pre_g.shape[0]):
        x2d, w_in_b, w_bg_b, wd2_b = _ffn(
            x2d, ffn1_pre_g[l][None], bf(ffn1_w_gate[l]), bf(ffn1_w_up[l]), bf(ffn1_w_down[l]),
            ffn1_post_g[l][None], tm=FFN_ROWS, riders=(w_in[l], w_branch_gate[l], ffn2_w_down[l]))
        r3 = lambda t: t.reshape(b, s, -1)
        wax = bf(jnp.concatenate([lru_w_a[l], lru_w_x[l]], axis=-1))
        q_da, k_da, v_da, q_ca, gates, o_lru, w_da_b, w_lru_b, w_ca_b, w_mix_b, w_kv_b = _proj_lru(
            r3(x2d), mix_pre_g[l][None], w_in_b, w_bg_b, b_branch_gate[l][None],
            lru_conv_w[l], lru_conv_b[l][None], wax, lru_b_a[l][None], lru_b_x[l][None], lru_lambda[l][None],
            tm=MIX_ROWS, riders=(w_da_out[l], w_lru_out[l], w_ca_out[l], w_mix_out[l], w_mem_kv[l]))
        o_da = _diff_attn(slopes, q_da, k_da, v_da, da_lambda_q1[l][None], da_lambda_k1[l][None],
                          da_lambda_q2[l][None], da_lambda_k2[l][None], da_head_g[l][None], tq=ATTN_ROWS)
        km, vm = _mem_kv(mem, mem_g[l][None], w_kv_b)
        x3, wg2_b, wu2_b = _merge(
            r3(x2d), o_da, o_lru, q_ca, km, vm, gates, w_da_b, w_lru_b, w_ca_b, w_mix_b, mix_post_g[l][None],
            tm=MIX_ROWS, riders=(ffn2_w_gate[l], ffn2_w_up[l]))
        x2d, = _ffn(x3.reshape(n, d), ffn2_pre_g[l][None], wg2_b, wu2_b, wd2_b, ffn2_post_g[l][None],
                    tm=FFN_ROWS)
    return x2d.reshape(b, s, d)
```

```python
import functools
import math

import jax
import jax.numpy as jnp
from jax import lax
from jax.experimental import pallas as pl
from jax.experimental.pallas import tpu as pltpu

D_MODEL = 1024
N_MEM = 256
DA_HEADS = 8
DA_QK_DIM = 64
DA_V_DIM = 128
LRU_BLOCKS = 8
LRU_BLOCK = 128
CONV_WIDTH = 4
LRU_C = 8.0
CA_HEADS = 4
CA_HEAD_DIM = 256
D_FF = 2816
MACARON_WEIGHT = 0.5
EPS = 1e-6
LAM_INIT = 0.8 - 0.6 * math.exp(-0.3 * 0)
LOG2E = math.log2(math.e)

SUBLANES = 8
LANES = 128
ROW_GROUP = 16
FF_CHUNK = 256
FFN_ROWS = 1024
FFN_SPLIT = 2
MIX_ROWS = 512
ATTN_ROWS = 512
HEADS_PER_STEP = 2
VMEM_LIMIT = 56 * 1024 * 1024

F32 = jnp.float32
BF16 = jnp.bfloat16


def _rms(xf, g):
    return xf * lax.rsqrt(jnp.mean(xf * xf, axis=-1, keepdims=True) + EPS) * g


def _dot(a, b):
    return jnp.dot(a, b, preferred_element_type=F32)


def _dot_nt(a, b):
    return lax.dot_general(a, b, (((1,), (1,)), ((), ())), preferred_element_type=F32)


def _const_spec(shape):
    nd = len(shape)
    return pl.BlockSpec(shape, lambda *_: (0,) * nd, pipeline_mode=pl.Buffered(1))


def _params(sem):
    return pltpu.CompilerParams(dimension_semantics=sem, vmem_limit_bytes=VMEM_LIMIT)


def _rider_specs(weights, n_steps, step_of):
    in_specs, out_specs, out_shapes = [], [], []
    for w in weights:
        rows, cols = w.shape
        spec = pl.BlockSpec((rows // n_steps, cols), lambda *g: (step_of(*g), 0))
        in_specs.append(spec)
        out_specs.append(spec)
        out_shapes.append(jax.ShapeDtypeStruct(w.shape, BF16))
    return in_specs, out_specs, out_shapes


def _cast_riders(in_refs, out_refs):
    for w_ref, o_ref in zip(in_refs, out_refs):
        o_ref[...] = w_ref[...].astype(BF16)


def _ffn_kernel(x_ref, pre_g_ref, wg_ref, wu_ref, wd_ref, post_g_ref, *rest, n_riders):
    o_ref, act_ref = rest[n_riders], rest[-1]
    _cast_riders(rest[:n_riders], rest[n_riders + 1:-1])
    half = x_ref.shape[0] // FFN_SPLIT
    for part in range(FFN_SPLIT):
        rows = slice(part * half, (part + 1) * half)
        x = x_ref[rows, :]
        rinv = lax.rsqrt(jnp.mean(x * x, axis=-1, keepdims=True) + EPS)
        xg = (x * pre_g_ref[...]).astype(BF16)
        for c in range(D_FF // FF_CHUNK):
            sl = slice(c * FF_CHUNK, (c + 1) * FF_CHUNK)
            a = rinv * _dot(xg, wg_ref[:, sl])
            b = rinv * _dot(xg, wu_ref[:, sl])
            act_ref[rows, sl] = (a * jax.nn.sigmoid(a) * b).astype(BF16)
        f = _dot(act_ref[rows, :], wd_ref[...])
        o_ref[rows, :] = x + MACARON_WEIGHT * _rms(f, post_g_ref[...])


def _ffn(x2d, pre_g, wg, wu, wd, post_g, tm, riders=()):
    n, d = x2d.shape
    row = pl.BlockSpec((tm, d), lambda i: (i, 0))
    r_in, r_out, r_shapes = _rider_specs(riders, n // tm, lambda i: i)
    return pl.pallas_call(
        functools.partial(_ffn_kernel, n_riders=len(riders)),
        grid=(n // tm,),
        in_specs=[row, _const_spec((1, d)), _const_spec((d, D_FF)), _const_spec((d, D_FF)),
                  _const_spec((D_FF, d)), _const_spec((1, d))] + r_in,
        out_specs=[row] + r_out,
        out_shape=[jax.ShapeDtypeStruct((n, d), F32)] + r_shapes,
        scratch_shapes=[pltpu.VMEM((tm, D_FF), BF16)],
        compiler_params=_params(("parallel",)),
        name="ffn",
    )(x2d, pre_g, wg, wu, wd, post_g, *riders)


def _gelu_tanh(y):
    k = -2.0 * math.sqrt(2.0 / math.pi) * LOG2E
    return y / (1.0 + jnp.exp2(y * (k + (0.044715 * k) * (y * y))))


def _proj_lru_kernel(x_ref, g_ref, w_in_ref, w_bg_ref, b_bg_ref, cw_ref, cb_ref, wax_ref, ba_ref, bx_ref,
                     lam_ref, *rest, tm, n_riders):
    d = D_MODEL
    q_ref, k_ref, v_ref, qc_ref, gate_ref, ol_ref = rest[n_riders:n_riders + 6]
    tail_ref, carry_ref = rest[-2:]
    _cast_riders(rest[:n_riders], rest[n_riders + 6:-2])

    @pl.when(pl.program_id(1) == 0)
    def _():
        tail_ref[...] = jnp.zeros(tail_ref.shape, F32)
        carry_ref[...] = jnp.zeros(carry_ref.shape, F32)

    h = _rms(x_ref[0], g_ref[...]).astype(BF16)
    x_l = _dot(h, w_in_ref[:, 3 * d:4 * d])
    y_l = _dot(h, w_in_ref[:, 4 * d:5 * d])

    def proj_out(j, o_ref, scale):
        p = _dot(h, w_in_ref[:, j * d:(j + 1) * d])
        o_ref[0] = (p if scale == 1.0 else p * scale).astype(o_ref.dtype)

    def gate_out(j):
        sl = slice(j * d, (j + 1) * d)
        z = _dot(h, w_bg_ref[:, sl]) + b_bg_ref[:, sl]
        gate_ref[0, :, sl] = jax.nn.sigmoid(z).astype(gate_ref.dtype)

    other = [functools.partial(proj_out, 0, q_ref, DA_QK_DIM ** -0.5 * LOG2E),
             functools.partial(proj_out, 1, k_ref, 1.0),
             functools.partial(proj_out, 2, v_ref, 1.0),
             functools.partial(proj_out, 5, qc_ref, CA_HEAD_DIM ** -0.5),
             functools.partial(gate_out, 0), functools.partial(gate_out, 1), functools.partial(gate_out, 2)]

    nlam = -lam_ref[...]
    softplus = jnp.maximum(nlam, 0.0) + jnp.log1p(jnp.exp(-jnp.abs(nlam)))
    decay_rate = (-LRU_C * LOG2E) * softplus
    row8 = lax.broadcasted_iota(jnp.int32, (SUBLANES, LRU_BLOCK), 0)
    for n in range(LRU_BLOCKS):
        sl = slice(n * LRU_BLOCK, (n + 1) * LRU_BLOCK)
        groups = [slice(g * SUBLANES, (g + 1) * SUBLANES) for g in range(tm // SUBLANES)]
        taps = [cw_ref[t:t + 1, sl] for t in range(CONV_WIDTH)]
        bias_c, bias_a, bias_x, rate = cb_ref[:, sl], ba_ref[:, sl], bx_ref[:, sl], decay_rate[:, sl]

        prev_rolled = [pltpu.roll(tail_ref[:, sl], back, axis=0) for back in range(1, CONV_WIDTH)]
        xcs = []
        for rows in groups:
            cur = x_l[rows, sl]
            xc = bias_c + cur * taps[CONV_WIDTH - 1]
            for back in range(1, CONV_WIDTH):
                rolled = pltpu.roll(cur, back, axis=0)
                xc = xc + jnp.where(row8 < back, prev_rolled[back - 1], rolled) * taps[CONV_WIDTH - 1 - back]
                prev_rolled[back - 1] = rolled
            xcs.append(xc)
        tail_ref[:, sl] = x_l[groups[-1], sl]

        z = _dot(jnp.concatenate(xcs, axis=0).astype(BF16), wax_ref[n])

        carry = carry_ref[:, sl]
        outs = []
        for rows, xc in zip(groups, xcs):
            r = jax.nn.sigmoid(z[rows, :LRU_BLOCK] + bias_a)
            i = jax.nn.sigmoid(z[rows, LRU_BLOCK:] + bias_x)
            ag = jnp.exp2(r * rate)
            gap = 1.0 - ag * ag
            ug = jnp.where(gap > 0.0, gap * lax.rsqrt(gap), 0.0) * (i * xc)
            for dist in (1, 2, 4):
                ok = row8 >= dist
                a_prev = jnp.where(ok, pltpu.roll(ag, dist, axis=0), 1.0)
                u_prev = jnp.where(ok, pltpu.roll(ug, dist, axis=0), 0.0)
                ug = ag * u_prev + ug
                ag = ag * a_prev
            hg = ag * carry + ug
            carry = jnp.broadcast_to(hg[SUBLANES - 1:SUBLANES, :], (SUBLANES, LRU_BLOCK))
            outs.append(hg * _gelu_tanh(y_l[rows, sl]))
        carry_ref[:, sl] = carry
        ol_ref[0, :, sl] = jnp.concatenate(outs, axis=0).astype(ol_ref.dtype)
        if n < len(other):
            other[n]()


def _proj_lru(x, g, w_in, w_bg, b_bg, conv_w, conv_b, wax, b_a, b_x, lam, tm, riders=()):
    b, s, d = x.shape
    row = pl.BlockSpec((1, tm, d), lambda bi, si: (bi, si, 0))
    bf = jax.ShapeDtypeStruct((b, s, d), BF16)
    tiles = s // tm
    r_in, r_out, r_shapes = _rider_specs(riders, b * tiles, lambda bi, si: bi * tiles + si)
    return pl.pallas_call(
        functools.partial(_proj_lru_kernel, tm=tm, n_riders=len(riders)),
        grid=(b, tiles),
        in_specs=[row, _const_spec((1, d)), _const_spec(w_in.shape), _const_spec(w_bg.shape),
                  _const_spec((1, 3 * d)), _const_spec((CONV_WIDTH, d)), _const_spec((1, d)),
                  _const_spec(wax.shape), _const_spec((1, d)), _const_spec((1, d)), _const_spec((1, d))] + r_in,
        out_specs=[row, row, row, row, pl.BlockSpec((1, tm, 3 * d), lambda bi, si: (bi, si, 0)), row] + r_out,
        out_shape=[bf, bf, bf, bf, jax.ShapeDtypeStruct((b, s, 3 * d), BF16), bf] + r_shapes,
        scratch_shapes=[pltpu.VMEM((SUBLANES, d), F32), pltpu.VMEM((SUBLANES, d), F32)],
        compiler_params=_params(("parallel", "arbitrary")),
        name="proj_lru",
    )(x, g, w_in, w_bg, b_bg, conv_w, conv_b, wax, b_a, b_x, lam, *riders)


def _diff_attn_kernel(slopes_ref, zero_ref, q_ref, k_ref, v_ref, lq1_ref, lk1_ref, lq2_ref, lk2_ref, hg_ref, o_ref,
                      vaug_ref, qm_ref, m_ref, acc_ref, s0_ref, s1_ref, p0_ref, p1_ref, a0_ref, a1_ref, *, tq):
    seq = q_ref.shape[1]
    dv = DA_V_DIM
    n_tiles = seq // tq
    head_lanes = [slice(hd * dv, (hd + 1) * dv) for hd in range(HEADS_PER_STEP)]
    slopes = [slopes_ref[pl.program_id(1) * HEADS_PER_STEP + hd] * LOG2E for hd in range(HEADS_PER_STEP)]
    n_col_tiles = tq // LANES
    half = tq // 2
    s_refs, p_refs, alpha_refs = (s0_ref, s1_ref), (p0_ref, p1_ref), (a0_ref, a1_ref)
    dyn0 = zero_ref[0]

    for hd in range(HEADS_PER_STEP):
        vaug_ref[hd, :, :dv] = v_ref[0, :, head_lanes[hd]]
        vaug_ref[hd, :, dv:] = jnp.ones((seq, dv), BF16)
    lam = (jnp.exp(jnp.sum(lq1_ref[...] * lk1_ref[...], axis=-1, keepdims=True))
           - jnp.exp(jnp.sum(lq2_ref[...] * lk2_ref[...], axis=-1, keepdims=True)) + LAM_INIT)

    lane = lax.broadcasted_iota(jnp.int32, (1, 2 * DA_QK_DIM), 1)
    col = lax.broadcasted_iota(jnp.int32, (1, tq), 1)
    rr = lax.broadcasted_iota(jnp.int32, (ROW_GROUP, LANES), 0)
    cc = lax.broadcasted_iota(jnp.int32, (ROW_GROUP, LANES), 1)
    pieces = [(hd, qi, j) for hd in range(HEADS_PER_STEP) for qi in range(n_tiles) for j in range(qi + 1)]

    def load_queries(hd, qi):
        q = q_ref[0, qi * tq:(qi + 1) * tq, head_lanes[hd]]
        zero = jnp.zeros_like(q)
        qbuf = (hd * n_tiles + qi) % 2
        for part in range(2):
            lo = 2 * part * half
            qm_ref[qbuf, lo:lo + half, :] = jnp.where(lane < DA_QK_DIM, q, zero)[part * half:(part + 1) * half]
            qm_ref[qbuf, lo + half:lo + tq, :] = jnp.where(lane >= DA_QK_DIM, q, zero)[part * half:(part + 1) * half]

    def start_scores(n):
        hd, qi, j = pieces[n]
        k0 = j * tq
        qbuf = (hd * n_tiles + qi) % 2
        if j == qi:
            s_refs[n % 2][:tq, :half] = _dot_nt(qm_ref[qbuf, :tq, :], k_ref[0, k0:k0 + half, head_lanes[hd]])
            s_refs[n % 2][tq:, :] = _dot_nt(qm_ref[qbuf, tq:, :], k_ref[0, k0:k0 + tq, head_lanes[hd]])
        else:
            s_refs[n % 2][...] = _dot_nt(qm_ref[qbuf], k_ref[0, k0:k0 + tq, head_lanes[hd]])

    def softmax_pv(n):
        hd, qi, j = pieces[n]
        buf = n % 2
        first, diagonal = j == 0, j == qi
        bias = slopes[hd] * (col + (j - qi) * tq).astype(F32)
        groups = [(pl.ds(pl.multiple_of(r0 + dyn0, ROW_GROUP), ROW_GROUP), (r0 // tq) * half + r0 % half,
                   n_col_tiles // 2 if diagonal and r0 < tq else n_col_tiles)
                  for r0 in range(0, 2 * tq, ROW_GROUP)]

        def score_tile(rows, rq, c):
            c0 = c * LANES
            t = s_refs[buf][rows, c0:c0 + LANES] + bias[:, c0:c0 + LANES]
            if diagonal and c0 + LANES - 1 > rq:
                t = jnp.where(rr + rq >= cc + c0, t, -jnp.inf)
            return t

        for rows, rq, n_cols in groups:
            tiles = [score_tile(rows, rq, c) for c in range(n_cols)]
            mx = jnp.max(functools.reduce(jnp.maximum, tiles), axis=-1, keepdims=True)
            if first:
                m_new = jnp.broadcast_to(mx, (ROW_GROUP, LANES))
            else:
                m_old = m_ref[rows, :]
                m_new = jnp.maximum(m_old, mx)
                alpha_refs[buf][rows, :] = jnp.exp2(m_old - m_new)
            m_ref[rows, :] = m_new
        for rows, rq, n_cols in groups:
            m_new = m_ref[rows, :]
            for c in range(n_cols):
                p_refs[buf][rows, c * LANES:(c + 1) * LANES] = jnp.exp2(score_tile(rows, rq, c) - m_new).astype(BF16)
        k0 = j * tq
        if diagonal:
            products = [(slice(0, tq), _dot(p_refs[buf][:tq, :half], vaug_ref[hd, k0:k0 + half, :])),
                        (slice(tq, 2 * tq), _dot(p_refs[buf][tq:, :], vaug_ref[hd, k0:k0 + tq, :]))]
        else:
            products = [(slice(0, 2 * tq), _dot(p_refs[buf][...], vaug_ref[hd, k0:k0 + tq, :]))]
        for part_rows, pv in products:
            if first:
                acc_ref[part_rows, :] = pv
            else:
                alpha = alpha_refs[buf][part_rows, :]
                acc_ref[part_rows, :dv] = alpha * acc_ref[part_rows, :dv] + pv[:, :dv]
                acc_ref[part_rows, dv:] = alpha * acc_ref[part_rows, dv:] + pv[:, dv:]

    def finalize(hd, qi):
        for part in range(2):
            m1 = slice(2 * part * half, (2 * part + 1) * half)
            m2 = slice((2 * part + 1) * half, (2 * part + 2) * half)
            o = (acc_ref[m1, :dv] / acc_ref[m1, dv:]) - lam * (acc_ref[m2, :dv] / acc_ref[m2, dv:])
            out_rows = slice(qi * tq + part * half, qi * tq + (part + 1) * half)
            o_ref[0, out_rows, head_lanes[hd]] = (_rms(o, hg_ref[...]) * (1.0 - LAM_INIT)).astype(o_ref.dtype)

    load_queries(0, 0)
    start_scores(0)
    for n, (hd, qi, j) in enumerate(pieces):
        if n + 1 < len(pieces):
            if pieces[n + 1][:2] != (hd, qi):
                load_queries(*pieces[n + 1][:2])
            start_scores(n + 1)
        softmax_pv(n)
        if j == qi:
            finalize(hd, qi)


def _diff_attn(slopes, q, k, v, lq1, lk1, lq2, lk2, head_g, tq):
    b, s, d = q.shape
    dv = DA_V_DIM
    head = pl.BlockSpec((1, s, HEADS_PER_STEP * dv), lambda bi, hi: (bi, 0, hi))
    vec = lambda w: pl.BlockSpec((1, w), lambda bi, hi: (0, 0))
    return pl.pallas_call(
        functools.partial(_diff_attn_kernel, tq=tq),
        grid=(b, DA_HEADS // HEADS_PER_STEP),
        in_specs=[pl.BlockSpec(memory_space=pltpu.SMEM), pl.BlockSpec(memory_space=pltpu.SMEM), head, head, head,
                  vec(DA_QK_DIM), vec(DA_QK_DIM), vec(DA_QK_DIM), vec(DA_QK_DIM), vec(dv)],
        out_specs=head,
        out_shape=jax.ShapeDtypeStruct((b, s, d), BF16),
        scratch_shapes=[pltpu.VMEM((HEADS_PER_STEP, s, 2 * dv), BF16),
                        pltpu.VMEM((2, 2 * tq, dv), BF16),
                        pltpu.VMEM((2 * tq, LANES), F32),
                        pltpu.VMEM((2 * tq, 2 * dv), F32),
                        pltpu.VMEM((2 * tq, tq), F32), pltpu.VMEM((2 * tq, tq), F32),
                        pltpu.VMEM((2 * tq, tq), BF16), pltpu.VMEM((2 * tq, tq), BF16),
                        pltpu.VMEM((2 * tq, LANES), F32), pltpu.VMEM((2 * tq, LANES), F32)],
        compiler_params=_params(("parallel", "parallel")),
        name="diff_attn",
    )(slopes, jnp.zeros((1,), jnp.int32), q, k, v, lq1, lk1, lq2, lk2, head_g)


def _mem_kv_kernel(mem_ref, g_ref, w_ref, km_ref, vm_ref):
    mn = _rms(mem_ref[0], g_ref[...]).astype(BF16)
    d = D_MODEL
    km_ref[0] = _dot(mn, w_ref[:, :d]).astype(km_ref.dtype)
    vm_ref[0] = _dot(mn, w_ref[:, d:]).astype(vm_ref.dtype)


def _mem_kv(mem, g, w):
    b, m, d = mem.shape
    blk = pl.BlockSpec((1, m, d), lambda bi: (bi, 0, 0))
    out = jax.ShapeDtypeStruct((b, m, d), BF16)
    return pl.pallas_call(
        _mem_kv_kernel,
        grid=(b,),
        in_specs=[blk, _const_spec((1, d)), _const_spec(w.shape)],
        out_specs=[blk, blk],
        out_shape=[out, out],
        compiler_params=_params(("parallel",)),
        name="mem_kv",
    )(mem, g, w)


def _merge_kernel(x_ref, oda_ref, olru_ref, qc_ref, km_ref, vm_ref, gate_ref,
                  wda_ref, wlru_ref, wca_ref, wmix_ref, g_ref, *rest, n_riders):
    d = D_MODEL
    o_ref, oca_ref = rest[n_riders], rest[-1]
    _cast_riders(rest[:n_riders], rest[n_riders + 1:-1])
    qc = qc_ref[0]
    heads = [slice(hh * CA_HEAD_DIM, (hh + 1) * CA_HEAD_DIM) for hh in range(CA_HEADS)]
    scores = [_dot_nt(qc[:, sl], km_ref[0, :, sl]) for sl in heads]
    y_da = _dot(oda_ref[0], wda_ref[...])
    for sl, s in zip(heads, scores):
        p = jnp.exp(s - jnp.max(s, axis=-1, keepdims=True))
        o = _dot(p.astype(BF16), vm_ref[0, :, sl]) / jnp.sum(p, axis=-1, keepdims=True)
        oca_ref[:, sl] = o.astype(BF16)
    merged = (gate_ref[0, :, 0:d].astype(F32) * y_da
              + gate_ref[0, :, d:2 * d].astype(F32) * _dot(olru_ref[0], wlru_ref[...])
              + gate_ref[0, :, 2 * d:].astype(F32) * _dot(oca_ref[...], wca_ref[...]))
    out = _dot(merged.astype(BF16), wmix_ref[...])
    o_ref[0] = x_ref[0] + _rms(out, g_ref[...])


def _merge(x, o_da, o_lru, q_ca, km, vm, gates, w_da, w_lru, w_ca, w_mix, g, tm, riders=()):
    b, s, d = x.shape
    row = pl.BlockSpec((1, tm, d), lambda bi, si: (bi, si, 0))
    memspec = pl.BlockSpec((1, N_MEM, d), lambda bi, si: (bi, 0, 0))
    wspec = _const_spec((d, d))
    tiles = s // tm
    r_in, r_out, r_shapes = _rider_specs(riders, b * tiles, lambda bi, si: bi * tiles + si)
    return pl.pallas_call(
        functools.partial(_merge_kernel, n_riders=len(riders)),
        grid=(b, tiles),
        in_specs=[row, row, row, row, memspec, memspec,
                  pl.BlockSpec((1, tm, 3 * d), lambda bi, si: (bi, si, 0)),
                  wspec, wspec, wspec, wspec, _const_spec((1, d))] + r_in,
        out_specs=[row] + r_out,
        out_shape=[jax.ShapeDtypeStruct((b, s, d), F32)] + r_shapes,
        scratch_shapes=[pltpu.VMEM((tm, d), BF16)],
        compiler_params=_params(("parallel", "parallel")),
        name="merge",
    )(x, o_da, o_lru, q_ca, km, vm, gates, w_da, w_lru, w_ca, w_mix, g, *riders)


def kernel(x, mem, ffn1_pre_g, ffn1_w_gate, ffn1_w_up, ffn1_w_down, ffn1_post_g, mix_pre_g, w_in, da_lambda_q1, da_lambda_k1, da_lambda_q2, da_lambda_k2, da_head_g, w_da_out, lru_conv_w, lru_conv_b, lru_w_a, lru_b_a, lru_w_x, lru_b_x, lru_lambda, w_lru_out, mem_g, w_mem_kv, w_ca_out, w_branch_gate, b_branch_gate, w_mix_out, mix_post_g, ffn2_pre_g, ffn2_w_gate, ffn2_w_up, ffn2_w_down, ffn2_post_g):
    b, s, d = x.shape
    n = b * s
    bf = lambda w: w.astype(BF16)
    slopes = jnp.exp2(-8.0 * jnp.arange(1, DA_HEADS + 1, dtype=F32) / DA_HEADS)

    x2d = x.reshape(n, d)
    for l in range(ffn1_pre_g.shape[0]):
        x2d, w_in_b, w_bg_b, wd2_b = _ffn(
            x2d, ffn1_pre_g[l][None], bf(ffn1_w_gate[l]), bf(ffn1_w_up[l]), bf(ffn1_w_down[l]),
            ffn1_post_g[l][None], tm=FFN_ROWS, riders=(w_in[l], w_branch_gate[l], ffn2_w_down[l]))
        r3 = lambda t: t.reshape(b, s, -1)
        wax = bf(jnp.concatenate([lru_w_a[l], lru_w_x[l]], axis=-1))
        q_da, k_da, v_da, q_ca, gates, o_lru, w_da_b, w_lru_b, w_ca_b, w_mix_b, w_kv_b = _proj_lru(
            r3(x2d), mix_pre_g[l][None], w_in_b, w_bg_b, b_branch_gate[l][None],
            lru_conv_w[l], lru_conv_b[l][None], wax, lru_b_a[l][None], lru_b_x[l][None], lru_lambda[l][None],
            tm=MIX_ROWS, riders=(w_da_out[l], w_lru_out[l], w_ca_out[l], w_mix_out[l], w_mem_kv[l]))
        o_da = _diff_attn(slopes, q_da, k_da, v_da, da_lambda_q1[l][None], da_lambda_k1[l][None],
                          da_lambda_q2[l][None], da_lambda_k2[l][None], da_head_g[l][None], tq=ATTN_ROWS)
        km, vm = _mem_kv(mem, mem_g[l][None], w_kv_b)
        x3, wg2_b, wu2_b = _merge(
            r3(x2d), o_da, o_lru, q_ca, km, vm, gates, w_da_b, w_lru_b, w_ca_b, w_mix_b, mix_post_g[l][None],
            tm=MIX_ROWS, riders=(ffn2_w_gate[l], ffn2_w_up[l]))
        x2d, = _ffn(x3.reshape(n, d), ffn2_pre_g[l][None], wg2_b, wu2_b, wd2_b, ffn2_post_g[l][None],
                    tm=FFN_ROWS)
    return x2d.reshape(b, s, d)
```

```python
import functools
import math

import jax
import jax.numpy as jnp
from jax import lax
from jax.experimental import pallas as pl
from jax.experimental.pallas import tpu as pltpu

D_MODEL = 1024
N_MEM = 256
DA_HEADS = 8
DA_QK_DIM = 64
DA_V_DIM = 128
LRU_BLOCKS = 8
LRU_BLOCK = 128
CONV_WIDTH = 4
LRU_C = 8.0
CA_HEADS = 4
CA_HEAD_DIM = 256
D_FF = 2816
MACARON_WEIGHT = 0.5
EPS = 1e-6
LAM_INIT = 0.8 - 0.6 * math.exp(-0.3 * 0)
LOG2E = math.log2(math.e)

SUBLANES = 8
LANES = 128
ROW_GROUP = 16
FF_CHUNK = 256
FFN_ROWS = 1024
FFN_SPLIT = 2
MIX_ROWS = 512
ATTN_ROWS = 512
HEADS_PER_STEP = 2
VMEM_LIMIT = 56 * 1024 * 1024

F32 = jnp.float32
BF16 = jnp.bfloat16


def _rms(xf, g):
    return xf * lax.rsqrt(jnp.mean(xf * xf, axis=-1, keepdims=True) + EPS) * g


def _dot(a, b):
    return jnp.dot(a, b, preferred_element_type=F32)


def _dot_nt(a, b):
    return lax.dot_general(a, b, (((1,), (1,)), ((), ())), preferred_element_type=F32)


def _const_spec(shape):
    nd = len(shape)
    return pl.BlockSpec(shape, lambda *_: (0,) * nd, pipeline_mode=pl.Buffered(1))


def _params(sem):
    return pltpu.CompilerParams(dimension_semantics=sem, vmem_limit_bytes=VMEM_LIMIT)


def _rider_specs(weights, n_steps, step_of):
    in_specs, out_specs, out_shapes = [], [], []
    for w in weights:
        rows, cols = w.shape
        spec = pl.BlockSpec((rows // n_steps, cols), lambda *g: (step_of(*g), 0))
        in_specs.append(spec)
        out_specs.append(spec)
        out_shapes.append(jax.ShapeDtypeStruct(w.shape, BF16))
    return in_specs, out_specs, out_shapes


def _cast_riders(in_refs, out_refs):
    for w_ref, o_ref in zip(in_refs, out_refs):
        o_ref[...] = w_ref[...].astype(BF16)


def _ffn_kernel(x_ref, pre_g_ref, wg_ref, wu_ref, wd_ref, post_g_ref, *rest, n_riders):
    o_ref, act_ref = rest[n_riders], rest[-1]
    _cast_riders(rest[:n_riders], rest[n_riders + 1:-1])
    half = x_ref.shape[0] // FFN_SPLIT
    for part in range(FFN_SPLIT):
        rows = slice(part * half, (part + 1) * half)
        x = x_ref[rows, :]
        rinv = lax.rsqrt(jnp.mean(x * x, axis=-1, keepdims=True) + EPS)
        xg = (x * pre_g_ref[...]).astype(BF16)
        for c in range(D_FF // FF_CHUNK):
            sl = slice(c * FF_CHUNK, (c + 1) * FF_CHUNK)
            a = rinv * _dot(xg, wg_ref[:, sl])
            b = rinv * _dot(xg, wu_ref[:, sl])
            act_ref[rows, sl] = (a * jax.nn.sigmoid(a) * b).astype(BF16)
        f = _dot(act_ref[rows, :], wd_ref[...])
        o_ref[rows, :] = x + MACARON_WEIGHT * _rms(f, post_g_ref[...])


def _ffn(x2d, pre_g, wg, wu, wd, post_g, tm, riders=()):
    n, d = x2d.shape
    row = pl.BlockSpec((tm, d), lambda i: (i, 0))
    r_in, r_out, r_shapes = _rider_specs(riders, n // tm, lambda i: i)
    return pl.pallas_call(
        functools.partial(_ffn_kernel, n_riders=len(riders)),
        grid=(n // tm,),
        in_specs=[row, _const_spec((1, d)), _const_spec((d, D_FF)), _const_spec((d, D_FF)),
                  _const_spec((D_FF, d)), _const_spec((1, d))] + r_in,
        out_specs=[row] + r_out,
        out_shape=[jax.ShapeDtypeStruct((n, d), F32)] + r_shapes,
        scratch_shapes=[pltpu.VMEM((tm, D_FF), BF16)],
        compiler_params=_params(("parallel",)),
        name="ffn",
    )(x2d, pre_g, wg, wu, wd, post_g, *riders)


def _gelu_tanh(y):
    k = -2.0 * math.sqrt(2.0 / math.pi) * LOG2E
    return y / (1.0 + jnp.exp2(y * (k + (0.044715 * k) * (y * y))))


def _proj_lru_kernel(x_ref, g_ref, w_in_ref, w_bg_ref, b_bg_ref, cw_ref, cb_ref, wax_ref, ba_ref, bx_ref,
                     lam_ref, *rest, tm, n_riders):
    d = D_MODEL
    q_ref, k_ref, v_ref, qc_ref, gate_ref, ol_ref = rest[n_riders:n_riders + 6]
    tail_ref, carry_ref = rest[-2:]
    _cast_riders(rest[:n_riders], rest[n_riders + 6:-2])

    @pl.when(pl.program_id(1) == 0)
    def _():
        tail_ref[...] = jnp.zeros(tail_ref.shape, F32)
        carry_ref[...] = jnp.zeros(carry_ref.shape, F32)

    h = _rms(x_ref[0], g_ref[...]).astype(BF16)
    x_l = _dot(h, w_in_ref[:, 3 * d:4 * d])
    y_l = _dot(h, w_in_ref[:, 4 * d:5 * d])

    def proj_out(j, o_ref, scale):
        p = _dot(h, w_in_ref[:, j * d:(j + 1) * d])
        o_ref[0] = (p if scale == 1.0 else p * scale).astype(o_ref.dtype)

    def gate_out(j):
        sl = slice(j * d, (j + 1) * d)
        z = _dot(h, w_bg_ref[:, sl]) + b_bg_ref[:, sl]
        gate_ref[0, :, sl] = jax.nn.sigmoid(z).astype(gate_ref.dtype)

    other = [functools.partial(proj_out, 0, q_ref, DA_QK_DIM ** -0.5 * LOG2E),
             functools.partial(proj_out, 1, k_ref, 1.0),
             functools.partial(proj_out, 2, v_ref, 1.0),
             functools.partial(proj_out, 5, qc_ref, CA_HEAD_DIM ** -0.5),
             functools.partial(gate_out, 0), functools.partial(gate_out, 1), functools.partial(gate_out, 2)]

    nlam = -lam_ref[...]
    softplus = jnp.maximum(nlam, 0.0) + jnp.log1p(jnp.exp(-jnp.abs(nlam)))
    decay_rate = (-LRU_C * LOG2E) * softplus
    row8 = lax.broadcasted_iota(jnp.int32, (SUBLANES, LRU_BLOCK), 0)
    for n in range(LRU_BLOCKS):
        sl = slice(n * LRU_BLOCK, (n + 1) * LRU_BLOCK)
        groups = [slice(g * SUBLANES, (g + 1) * SUBLANES) for g in range(tm // SUBLANES)]
        taps = [cw_ref[t:t + 1, sl] for t in range(CONV_WIDTH)]
        bias_c, bias_a, bias_x, rate = cb_ref[:, sl], ba_ref[:, sl], bx_ref[:, sl], decay_rate[:, sl]

        prev_rolled = [pltpu.roll(tail_ref[:, sl], back, axis=0) for back in range(1, CONV_WIDTH)]
        xcs = []
        for rows in groups:
            cur = x_l[rows, sl]
            xc = bias_c + cur * taps[CONV_WIDTH - 1]
            for back in range(1, CONV_WIDTH):
                rolled = pltpu.roll(cur, back, axis=0)
                xc = xc + jnp.where(row8 < back, prev_rolled[back - 1], rolled) * taps[CONV_WIDTH - 1 - back]
                prev_rolled[back - 1] = rolled
            xcs.append(xc)
        tail_ref[:, sl] = x_l[groups[-1], sl]

        z = _dot(jnp.concatenate(xcs, axis=0).astype(BF16), wax_ref[n])

        carry = carry_ref[:, sl]
        outs = []
        for rows, xc in zip(groups, xcs):
            r = jax.nn.sigmoid(z[rows, :LRU_BLOCK] + bias_a)
            i = jax.nn.sigmoid(z[rows, LRU_BLOCK:] + bias_x)
            ag = jnp.exp2(r * rate)
            gap = 1.0 - ag * ag
            ug = jnp.where(gap > 0.0, gap * lax.rsqrt(gap), 0.0) * (i * xc)
            for dist in (1, 2, 4):
                ok = row8 >= dist
                a_prev = jnp.where(ok, pltpu.roll(ag, dist, axis=0), 1.0)
                u_prev = jnp.where(ok, pltpu.roll(ug, dist, axis=0), 0.0)
                ug = ag * u_prev + ug
                ag = ag * a_prev
            hg = ag * carry + ug
            carry = jnp.broadcast_to(hg[SUBLANES - 1:SUBLANES, :], (SUBLANES, LRU_BLOCK))
            outs.append(hg * _gelu_tanh(y_l[rows, sl]))
        carry_ref[:, sl] = carry
        ol_ref[0, :, sl] = jnp.concatenate(outs, axis=0).astype(ol_ref.dtype)
        if n < len(other):
            other[n]()


def _proj_lru(x, g, w_in, w_bg, b_bg, conv_w, conv_b, wax, b_a, b_x, lam, tm, riders=()):
    b, s, d = x.shape
    row = pl.BlockSpec((1, tm, d), lambda bi, si: (bi, si, 0))
    bf = jax.ShapeDtypeStruct((b, s, d), BF16)
    tiles = s // tm
    r_in, r_out, r_shapes = _rider_specs(riders, b * tiles, lambda bi, si: bi * tiles + si)
    return pl.pallas_call(
        functools.partial(_proj_lru_kernel, tm=tm, n_riders=len(riders)),
        grid=(b, tiles),
        in_specs=[row, _const_spec((1, d)), _const_spec(w_in.shape), _const_spec(w_bg.shape),
                  _const_spec((1, 3 * d)), _const_spec((CONV_WIDTH, d)), _const_spec((1, d)),
                  _const_spec(wax.shape), _const_spec((1, d)), _const_spec((1, d)), _const_spec((1, d))] + r_in,
        out_specs=[row, row, row, row, pl.BlockSpec((1, tm, 3 * d), lambda bi, si: (bi, si, 0)), row] + r_out,
        out_shape=[bf, bf, bf, bf, jax.ShapeDtypeStruct((b, s, 3 * d), BF16), bf] + r_shapes,
        scratch_shapes=[pltpu.VMEM((SUBLANES, d), F32), pltpu.VMEM((SUBLANES, d), F32)],
        compiler_params=_params(("parallel", "arbitrary")),
        name="proj_lru",
    )(x, g, w_in, w_bg, b_bg, conv_w, conv_b, wax, b_a, b_x, lam, *riders)


def _diff_attn_kernel(slopes_ref, zero_ref, q_ref, k_ref, v_ref, lq1_ref, lk1_ref, lq2_ref, lk2_ref, hg_ref, o_ref,
                      vaug_ref, qm_ref, m_ref, acc_ref, s0_ref, s1_ref, p0_ref, p1_ref, a0_ref, a1_ref, *, tq):
    seq = q_ref.shape[1]
    dv = DA_V_DIM
    n_tiles = seq // tq
    head_lanes = [slice(hd * dv, (hd + 1) * dv) for hd in range(HEADS_PER_STEP)]
    slopes = [slopes_ref[pl.program_id(1) * HEADS_PER_STEP + hd] * LOG2E for hd in range(HEADS_PER_STEP)]
    n_col_tiles = tq // LANES
    half = tq // 2
    s_refs, p_refs, alpha_refs = (s0_ref, s1_ref), (p0_ref, p1_ref), (a0_ref, a1_ref)
    dyn0 = zero_ref[0]

    for hd in range(HEADS_PER_STEP):
        vaug_ref[hd, :, :dv] = v_ref[0, :, head_lanes[hd]]
        vaug_ref[hd, :, dv:] = jnp.ones((seq, dv), BF16)
    lam = (jnp.exp(jnp.sum(lq1_ref[...] * lk1_ref[...], axis=-1, keepdims=True))
           - jnp.exp(jnp.sum(lq2_ref[...] * lk2_ref[...], axis=-1, keepdims=True)) + LAM_INIT)

    lane = lax.broadcasted_iota(jnp.int32, (1, 2 * DA_QK_DIM), 1)
    col = lax.broadcasted_iota(jnp.int32, (1, tq), 1)
    rr = lax.broadcasted_iota(jnp.int32, (ROW_GROUP, LANES), 0)
    cc = lax.broadcasted_iota(jnp.int32, (ROW_GROUP, LANES), 1)
    pieces = [(hd, qi, j) for hd in range(HEADS_PER_STEP) for qi in range(n_tiles) for j in range(qi + 1)]

    def load_queries(hd, qi):
        q = q_ref[0, qi * tq:(qi + 1) * tq, head_lanes[hd]]
        zero = jnp.zeros_like(q)
        qbuf = (hd * n_tiles + qi) % 2
        for part in range(2):
            lo = 2 * part * half
            qm_ref[qbuf, lo:lo + half, :] = jnp.where(lane < DA_QK_DIM, q, zero)[part * half:(part + 1) * half]
            qm_ref[qbuf, lo + half:lo + tq, :] = jnp.where(lane >= DA_QK_DIM, q, zero)[part * half:(part + 1) * half]

    def start_scores(n):
        hd, qi, j = pieces[n]
        k0 = j * tq
        qbuf = (hd * n_tiles + qi) % 2
        if j == qi:
            s_refs[n % 2][:tq, :half] = _dot_nt(qm_ref[qbuf, :tq, :], k_ref[0, k0:k0 + half, head_lanes[hd]])
            s_refs[n % 2][tq:, :] = _dot_nt(qm_ref[qbuf, tq:, :], k_ref[0, k0:k0 + tq, head_lanes[hd]])
        else:
            s_refs[n % 2][...] = _dot_nt(qm_ref[qbuf], k_ref[0, k0:k0 + tq, head_lanes[hd]])

    def softmax_pv(n):
        hd, qi, j = pieces[n]
        buf = n % 2
        first, diagonal = j == 0, j == qi
        bias = slopes[hd] * (col + (j - qi) * tq).astype(F32)
        groups = [(pl.ds(pl.multiple_of(r0 + dyn0, ROW_GROUP), ROW_GROUP), (r0 // tq) * half + r0 % half,
                   n_col_tiles // 2 if diagonal and r0 < tq else n_col_tiles)
                  for r0 in range(0, 2 * tq, ROW_GROUP)]

        def score_tile(rows, rq, c):
            c0 = c * LANES
            t = s_refs[buf][rows, c0:c0 + LANES] + bias[:, c0:c0 + LANES]
            if diagonal and c0 + LANES - 1 > rq:
                t = jnp.where(rr + rq >= cc + c0, t, -jnp.inf)
            return t

        for rows, rq, n_cols in groups:
            tiles = [score_tile(rows, rq, c) for c in range(n_cols)]
            mx = jnp.max(functools.reduce(jnp.maximum, tiles), axis=-1, keepdims=True)
            if first:
                m_new = jnp.broadcast_to(mx, (ROW_GROUP, LANES))
            else:
                m_old = m_ref[rows, :]
                m_new = jnp.maximum(m_old, mx)
                alpha_refs[buf][rows, :] = jnp.exp2(m_old - m_new)
            m_ref[rows, :] = m_new
        for rows, rq, n_cols in groups:
            m_new = m_ref[rows, :]
            for c in range(n_cols):
                p_refs[buf][rows, c * LANES:(c + 1) * LANES] = jnp.exp2(score_tile(rows, rq, c) - m_new).astype(BF16)
        k0 = j * tq
        if diagonal:
            products = [(slice(0, tq), _dot(p_refs[buf][:tq, :half], vaug_ref[hd, k0:k0 + half, :])),
                        (slice(tq, 2 * tq), _dot(p_refs[buf][tq:, :], vaug_ref[hd, k0:k0 + tq, :]))]
        else:
            products = [(slice(0, 2 * tq), _dot(p_refs[buf][...], vaug_ref[hd, k0:k0 + tq, :]))]
        for part_rows, pv in products:
            if first:
                acc_ref[part_rows, :] = pv
            else:
                alpha = alpha_refs[buf][part_rows, :]
                acc_ref[part_rows, :dv] = alpha * acc_ref[part_rows, :dv] + pv[:, :dv]
                acc_ref[part_rows, dv:] = alpha * acc_ref[part_rows, dv:] + pv[:, dv:]

    def finalize(hd, qi):
        for part in range(2):
            m1 = slice(2 * part * half, (2 * part + 1) * half)
            m2 = slice((2 * part + 1) * half, (2 * part + 2) * half)
            o = (acc_ref[m1, :dv] / acc_ref[m1, dv:]) - lam * (acc_ref[m2, :dv] / acc_ref[m2, dv:])
            out_rows = slice(qi * tq + part * half, qi * tq + (part + 1) * half)
            o_ref[0, out_rows, head_lanes[hd]] = (_rms(o, hg_ref[...]) * (1.0 - LAM_INIT)).astype(o_ref.dtype)

    load_queries(0, 0)
    start_scores(0)
    for n, (hd, qi, j) in enumerate(pieces):
        if n + 1 < len(pieces):
            if pieces[n + 1][:2] != (hd, qi):
                load_queries(*pieces[n + 1][:2])
            start_scores(n + 1)
        softmax_pv(n)
        if j == qi:
            finalize(hd, qi)


def _diff_attn(slopes, q, k, v, lq1, lk1, lq2, lk2, head_g, tq):
    b, s, d = q.shape
    dv = DA_V_DIM
    head = pl.BlockSpec((1, s, HEADS_PER_STEP * dv), lambda bi, hi: (bi, 0, hi))
    vec = lambda w: pl.BlockSpec((1, w), lambda bi, hi: (0, 0))
    return pl.pallas_call(
        functools.partial(_diff_attn_kernel, tq=tq),
        grid=(b, DA_HEADS // HEADS_PER_STEP),
        in_specs=[pl.BlockSpec(memory_space=pltpu.SMEM), pl.BlockSpec(memory_space=pltpu.SMEM), head, head, head,
                  vec(DA_QK_DIM), vec(DA_QK_DIM), vec(DA_QK_DIM), vec(DA_QK_DIM), vec(dv)],
        out_specs=head,
        out_shape=jax.ShapeDtypeStruct((b, s, d), BF16),
        scratch_shapes=[pltpu.VMEM((HEADS_PER_STEP, s, 2 * dv), BF16),
                        pltpu.VMEM((2, 2 * tq, dv), BF16),
                        pltpu.VMEM((2 * tq, LANES), F32),
                        pltpu.VMEM((2 * tq, 2 * dv), F32),
                        pltpu.VMEM((2 * tq, tq), F32), pltpu.VMEM((2 * tq, tq), F32),
                        pltpu.VMEM((2 * tq, tq), BF16), pltpu.VMEM((2 * tq, tq), BF16),
                        pltpu.VMEM((2 * tq, LANES), F32), pltpu.VMEM((2 * tq, LANES), F32)],
        compiler_params=_params(("parallel", "parallel")),
        name="diff_attn",
    )(slopes, jnp.zeros((1,), jnp.int32), q, k, v, lq1, lk1, lq2, lk2, head_g)


def _mem_kv_kernel(mem_ref, g_ref, w_ref, *rest, n_riders):
    km_ref, vm_ref = rest[n_riders:n_riders + 2]
    _cast_riders(rest[:n_riders], rest[n_riders + 2:])
    mn = _rms(mem_ref[0], g_ref[...]).astype(BF16)
    d = D_MODEL
    km_ref[0] = _dot(mn, w_ref[:, :d]).astype(km_ref.dtype)
    vm_ref[0] = _dot(mn, w_ref[:, d:]).astype(vm_ref.dtype)


def _mem_kv(mem, g, w, riders=()):
    b, m, d = mem.shape
    blk = pl.BlockSpec((1, m, d), lambda bi: (bi, 0, 0))
    out = jax.ShapeDtypeStruct((b, m, d), BF16)
    r_in, r_out, r_shapes = _rider_specs(riders, b, lambda bi: bi)
    return pl.pallas_call(
        functools.partial(_mem_kv_kernel, n_riders=len(riders)),
        grid=(b,),
        in_specs=[blk, _const_spec((1, d)), _const_spec(w.shape)] + r_in,
        out_specs=[blk, blk] + r_out,
        out_shape=[out, out] + r_shapes,
        compiler_params=_params(("parallel",)),
        name="mem_kv",
    )(mem, g, w, *riders)


def _merge_kernel(x_ref, oda_ref, olru_ref, qc_ref, km_ref, vm_ref, gate_ref,
                  wda_ref, wlru_ref, wca_ref, wmix_ref, g_ref, *rest, n_riders):
    d = D_MODEL
    o_ref, oca_ref = rest[n_riders], rest[-1]
    _cast_riders(rest[:n_riders], rest[n_riders + 1:-1])
    qc = qc_ref[0]
    heads = [slice(hh * CA_HEAD_DIM, (hh + 1) * CA_HEAD_DIM) for hh in range(CA_HEADS)]
    scores = [_dot_nt(qc[:, sl], km_ref[0, :, sl]) for sl in heads]
    y_da = _dot(oda_ref[0], wda_ref[...])
    for sl, s in zip(heads, scores):
        p = jnp.exp(s - jnp.max(s, axis=-1, keepdims=True))
        o = _dot(p.astype(BF16), vm_ref[0, :, sl]) / jnp.sum(p, axis=-1, keepdims=True)
        oca_ref[:, sl] = o.astype(BF16)
    merged = (gate_ref[0, :, 0:d].astype(F32) * y_da
              + gate_ref[0, :, d:2 * d].astype(F32) * _dot(olru_ref[0], wlru_ref[...])
              + gate_ref[0, :, 2 * d:].astype(F32) * _dot(oca_ref[...], wca_ref[...]))
    out = _dot(merged.astype(BF16), wmix_ref[...])
    o_ref[0] = x_ref[0] + _rms(out, g_ref[...])


def _merge(x, o_da, o_lru, q_ca, km, vm, gates, w_da, w_lru, w_ca, w_mix, g, tm, riders=()):
    b, s, d = x.shape
    row = pl.BlockSpec((1, tm, d), lambda bi, si: (bi, si, 0))
    memspec = pl.BlockSpec((1, N_MEM, d), lambda bi, si: (bi, 0, 0))
    wspec = _const_spec((d, d))
    tiles = s // tm
    r_in, r_out, r_shapes = _rider_specs(riders, b * tiles, lambda bi, si: bi * tiles + si)
    return pl.pallas_call(
        functools.partial(_merge_kernel, n_riders=len(riders)),
        grid=(b, tiles),
        in_specs=[row, row, row, row, memspec, memspec,
                  pl.BlockSpec((1, tm, 3 * d), lambda bi, si: (bi, si, 0)),
                  wspec, wspec, wspec, wspec, _const_spec((1, d))] + r_in,
        out_specs=[row] + r_out,
        out_shape=[jax.ShapeDtypeStruct((b, s, d), F32)] + r_shapes,
        scratch_shapes=[pltpu.VMEM((tm, d), BF16)],
        compiler_params=_params(("parallel", "parallel")),
        name="merge",
    )(x, o_da, o_lru, q_ca, km, vm, gates, w_da, w_lru, w_ca, w_mix, g, *riders)


def kernel(x, mem, ffn1_pre_g, ffn1_w_gate, ffn1_w_up, ffn1_w_down, ffn1_post_g, mix_pre_g, w_in, da_lambda_q1, da_lambda_k1, da_lambda_q2, da_lambda_k2, da_head_g, w_da_out, lru_conv_w, lru_conv_b, lru_w_a, lru_b_a, lru_w_x, lru_b_x, lru_lambda, w_lru_out, mem_g, w_mem_kv, w_ca_out, w_branch_gate, b_branch_gate, w_mix_out, mix_post_g, ffn2_pre_g, ffn2_w_gate, ffn2_w_up, ffn2_w_down, ffn2_post_g):
    b, s, d = x.shape
    n = b * s
    bf = lambda w: w.astype(BF16)
    slopes = jnp.exp2(-8.0 * jnp.arange(1, DA_HEADS + 1, dtype=F32) / DA_HEADS)

    x2d = x.reshape(n, d)
    for l in range(ffn1_pre_g.shape[0]):
        km, vm, wg1_b, wu1_b, wd1_b = _mem_kv(mem, mem_g[l][None], bf(w_mem_kv[l]),
                                              riders=(ffn1_w_gate[l], ffn1_w_up[l], ffn1_w_down[l]))
        x2d, w_in_b, w_bg_b, wd2_b = _ffn(
            x2d, ffn1_pre_g[l][None], wg1_b, wu1_b, wd1_b,
            ffn1_post_g[l][None], tm=FFN_ROWS, riders=(w_in[l], w_branch_gate[l], ffn2_w_down[l]))
        r3 = lambda t: t.reshape(b, s, -1)
        wax = bf(jnp.concatenate([lru_w_a[l], lru_w_x[l]], axis=-1))
        q_da, k_da, v_da, q_ca, gates, o_lru, w_da_b, w_lru_b, w_ca_b, w_mix_b = _proj_lru(
            r3(x2d), mix_pre_g[l][None], w_in_b, w_bg_b, b_branch_gate[l][None],
            lru_conv_w[l], lru_conv_b[l][None], wax, lru_b_a[l][None], lru_b_x[l][None], lru_lambda[l][None],
            tm=MIX_ROWS, riders=(w_da_out[l], w_lru_out[l], w_ca_out[l], w_mix_out[l]))
        o_da = _diff_attn(slopes, q_da, k_da, v_da, da_lambda_q1[l][None], da_lambda_k1[l][None],
                          da_lambda_q2[l][None], da_lambda_k2[l][None], da_head_g[l][None], tq=ATTN_ROWS)
        x3, wg2_b, wu2_b = _merge(
            r3(x2d), o_da, o_lru, q_ca, km, vm, gates, w_da_b, w_lru_b, w_ca_b, w_mix_b, mix_post_g[l][None],
            tm=MIX_ROWS, riders=(ffn2_w_gate[l], ffn2_w_up[l]))
        x2d, = _ffn(x3.reshape(n, d), ffn2_pre_g[l][None], wg2_b, wu2_b, wd2_b, ffn2_post_g[l][None],
                    tm=FFN_ROWS)
    return x2d.reshape(b, s, d)
```
